```python
import jax, jax.numpy as jnp
from jax import lax
import numpy as np

D_MODEL = 1024
BATCH = 4
SEQ = 4096
DEPTH = 4

CHUNK = 64
N_BRANCH = 4
BRANCH_W = 256
CONV_W = 3
POOL_WINDOWS = (2, 4, 8, 16)
POOL_GROUPS = len(POOL_WINDOWS)
POOL_GC = BRANCH_W // POOL_GROUPS
RET_HEADS = 4
RET_DK = BRANCH_W // RET_HEADS
RET_DV = BRANCH_W // RET_HEADS
ROPE_BASE = 10000.0
SGU_LEN = 128
SGU_GROUPS = 4
SGU_GC = BRANCH_W // SGU_GROUPS
D_FF = 2816
ALPHA = (2.0 * DEPTH) ** 0.25
BETA = (8.0 * DEPTH) ** -0.25
LN_EPS = 1e-5
GN_EPS = 1e-5

A_COLS = 3 * BRANCH_W
P_COLS = BRANCH_W
R_COLS = 4 * BRANCH_W
S_COLS = 2 * BRANCH_W
IN_COLS = A_COLS + P_COLS + R_COLS + S_COLS

kernel_name = "hybrid_gated_streaming_encoder"


def layer_norm(x, g, b):
    xf = x.astype(jnp.float32)
    mu = jnp.mean(xf, -1, keepdims=True)
    var = jnp.mean(jnp.square(xf - mu), -1, keepdims=True)
    return ((xf - mu) * lax.rsqrt(var + LN_EPS) * g + b).astype(x.dtype)


def swiglu_ffn(x, w1, w2):
    gate, up = jnp.split(x @ w1, 2, axis=-1)
    return (jax.nn.silu(gate) * up) @ w2


def short_conv_mixer(z, conv_w):
    bg, cg, xin = jnp.split(z, 3, axis=-1)
    y = lax.conv_general_dilated(
        cg * xin, conv_w[:, None, :].astype(z.dtype), window_strides=(1,),
        padding=[(CONV_W - 1, 0)], dimension_numbers=('NWC', 'WIO', 'NWC'),
        feature_group_count=BRANCH_W)
    return bg * y


def pool_mixer(z, pool_w, pool_scale):
    B, S, _ = z.shape
    zf = z.astype(jnp.float32).reshape(B, S, POOL_GROUPS, POOL_GC)
    cs = jnp.cumsum(zf, axis=1)
    t = jnp.arange(S)
    outs = []
    for g, w in enumerate(POOL_WINDOWS):
        c = cs[:, :, g]
        prev = jnp.pad(c, ((0, 0), (w, 0), (0, 0)))[:, :S]
        cnt = jnp.minimum(t + 1, w).astype(jnp.float32)[None, :, None]
        outs.append((c - prev) / cnt)
    pooled = jnp.stack(outs, axis=2)
    mixed = (pooled - zf).astype(z.dtype)
    y = jnp.einsum('bsgc,gcd->bsgd', mixed, pool_w)
    return y.reshape(B, S, BRANCH_W) * pool_scale


def rotary(x, pos):
    half = x.shape[-1] // 2
    inv = ROPE_BASE ** (-jnp.arange(half, dtype=jnp.float32) / half)
    ang = pos.astype(jnp.float32)[:, None] * inv[None, :]
    cos = jnp.cos(ang)[None, :, None, :]
    sin = jnp.sin(ang)[None, :, None, :]
    x1, x2 = x[..., :half], x[..., half:]
    return jnp.concatenate([x1 * cos - x2 * sin, x1 * sin + x2 * cos], axis=-1)


def retention_mixer(z, gn_g, gn_b):
    B, S, _ = z.shape
    H, N = RET_HEADS, S // CHUNK
    q, k, v, g = jnp.split(z.astype(jnp.float32), 4, axis=-1)
    pos = jnp.arange(S)
    q = rotary(q.reshape(B, S, H, RET_DK), pos)
    k = rotary(k.reshape(B, S, H, RET_DK), pos) * (RET_DK ** -0.5)
    v = v.reshape(B, S, H, RET_DV)

    def to_chunks(t):
        return t.reshape(B, N, CHUNK, H, -1).transpose(0, 3, 1, 2, 4)

    qc, kc, vc = to_chunks(q), to_chunks(k), to_chunks(v)
    log_gamma = jnp.log1p(-(2.0 ** (-5.0 - jnp.arange(H, dtype=jnp.float32))))
    idx = jnp.arange(CHUNK, dtype=jnp.float32)
    diff = idx[:, None] - idx[None, :]
    decay = jnp.where(diff >= 0,
                      jnp.exp(log_gamma[:, None, None] * jnp.maximum(diff, 0.0)), 0.0)
    scores = jnp.einsum('bhncd,bhnmd->bhncm', qc, kc) * decay[None, :, None]
    inner = jnp.einsum('bhncm,bhnme->bhnce', scores, vc)

    zeta = jnp.exp(log_gamma[:, None] * (CHUNK - 1 - idx)[None, :])
    xi = jnp.exp(log_gamma[:, None] * (idx + 1.0)[None, :])
    chunk_decay = jnp.exp(log_gamma * CHUNK)
    kv = jnp.einsum('bhncd,bhnce->bhnde', kc * zeta[None, :, None, :, None], vc)

    def step(state, kv_n):
        return state * chunk_decay[None, :, None, None] + kv_n, state

    _, prev = lax.scan(step, jnp.zeros((B, H, RET_DK, RET_DV), jnp.float32),
                       kv.transpose(2, 0, 1, 3, 4))
    prev = prev.transpose(1, 2, 0, 3, 4)
    cross = jnp.einsum('bhncd,bhnde->bhnce', qc * xi[None, :, None, :, None], prev)
    o = inner + cross

    mu = jnp.mean(o, -1, keepdims=True)
    var = jnp.mean(jnp.square(o - mu), -1, keepdims=True)
    o = (o - mu) * lax.rsqrt(var + GN_EPS)
    o = o.transpose(0, 2, 3, 1, 4).reshape(B, S, BRANCH_W) * gn_g + gn_b
    return (jax.nn.silu(g) * o).astype(z.dtype)


def spatial_gating_mixer(z, ln_g, ln_b, sgu_w, sgu_b):
    B, S, _ = z.shape
    u, v = jnp.split(jax.nn.gelu(z), 2, axis=-1)
    v = layer_norm(v, ln_g, ln_b)
    vc = v.reshape(B, S // SGU_LEN, SGU_LEN, SGU_GROUPS, SGU_GC)
    i = jnp.arange(SGU_LEN)
    mask = (i[None, :] // CHUNK) <= (i[:, None] // CHUNK)
    w = jnp.where(mask[None], sgu_w, jnp.zeros_like(sgu_w))
    mixed = jnp.einsum('gij,bnjgc->bnigc', w, vc) + sgu_b.T[None, None, :, :, None]
    return u * mixed.reshape(B, S, BRANCH_W)


def hybrid_mixer(h, w_in, conv_w, pool_w, pool_scale, ret_gn_g, ret_gn_b,
                 sgu_ln_g, sgu_ln_b, sgu_w, sgu_b, w_branch, w_gate, b_gate, w_out):
    B, S, D = h.shape
    z = h @ w_in
    za, zp, zr, zs = jnp.split(z, [A_COLS, A_COLS + P_COLS, A_COLS + P_COLS + R_COLS], axis=-1)
    ys = jnp.stack([short_conv_mixer(za, conv_w),
                    pool_mixer(zp, pool_w, pool_scale),
                    retention_mixer(zr, ret_gn_g, ret_gn_b),
                    spatial_gating_mixer(zs, sgu_ln_g, sgu_ln_b, sgu_w, sgu_b)], axis=2)
    branches = jnp.einsum('bsnw,nwd->bsnd', ys, w_branch)
    gates = jax.nn.sigmoid(h @ w_gate + b_gate).reshape(B, S, N_BRANCH, D)
    merged = jnp.sum(gates * branches, axis=2)
    return merged @ w_out


def setup_inputs(seed: int = 0) -> dict:
    key = jax.random.key(seed)
    ks = jax.random.split(key, 32)
    L, D, F, W = DEPTH, D_MODEL, D_FF, BRANCH_W
    f32 = jnp.float32

    def nrm(k, shape, scale):
        return jax.random.normal(k, shape, f32) * scale

    def gain(k, shape):
        return 1.0 + 0.02 * jax.random.normal(k, shape, f32)

    return {
        "x": jax.random.normal(ks[0], (BATCH, SEQ, D), f32),
        "ffn1_w1": nrm(ks[1], (L, D, 2 * F), D ** -0.5),
        "ffn1_w2": nrm(ks[2], (L, F, D), BETA * F ** -0.5),
        "ln1_g": gain(ks[3], (L, D)),
        "ln1_b": nrm(ks[4], (L, D), 0.01),
        "w_in": nrm(ks[5], (L, D, IN_COLS), D ** -0.5),
        "conv_w": nrm(ks[6], (L, CONV_W, W), CONV_W ** -0.5),
        "pool_w": nrm(ks[7], (L, POOL_GROUPS, POOL_GC, POOL_GC), POOL_GC ** -0.5),
        "pool_scale": gain(ks[8], (L, W)),
        "ret_gn_g": gain(ks[9], (L, W)),
        "ret_gn_b": nrm(ks[10], (L, W), 0.01),
        "sgu_ln_g": gain(ks[11], (L, W)),
        "sgu_ln_b": nrm(ks[12], (L, W), 0.01),
        "sgu_w": nrm(ks[13], (L, SGU_GROUPS, SGU_LEN, SGU_LEN), 0.5 * SGU_LEN ** -0.5),
        "sgu_b": gain(ks[14], (L, SGU_GROUPS, SGU_LEN)),
        "w_branch": nrm(ks[15], (L, N_BRANCH, W, D), W ** -0.5),
        "w_gate": nrm(ks[16], (L, D, N_BRANCH * D), D ** -0.5),
        "b_gate": nrm(ks[17], (L, N_BRANCH * D), 0.01),
        "w_out": nrm(ks[18], (L, D, D), BETA * D ** -0.5),
        "ln2_g": gain(ks[19], (L, D)),
        "ln2_b": nrm(ks[20], (L, D), 0.01),
        "ffn2_w1": nrm(ks[21], (L, D, 2 * F), D ** -0.5),
        "ffn2_w2": nrm(ks[22], (L, F, D), BETA * F ** -0.5),
        "ln3_g": gain(ks[23], (L, D)),
        "ln3_b": nrm(ks[24], (L, D), 0.01),
    }


def reference(x, ffn1_w1, ffn1_w2, ln1_g, ln1_b, w_in, conv_w, pool_w, pool_scale,
              ret_gn_g, ret_gn_b, sgu_ln_g, sgu_ln_b, sgu_w, sgu_b, w_branch, w_gate,
              b_gate, w_out, ln2_g, ln2_b, ffn2_w1, ffn2_w2, ln3_g, ln3_b):
    for l in range(DEPTH):
        x = layer_norm(ALPHA * x + 0.5 * swiglu_ffn(x, ffn1_w1[l], ffn1_w2[l]), ln1_g[l], ln1_b[l])
        m = hybrid_mixer(x, w_in[l], conv_w[l], pool_w[l], pool_scale[l], ret_gn_g[l], ret_gn_b[l],
                         sgu_ln_g[l], sgu_ln_b[l], sgu_w[l], sgu_b[l], w_branch[l], w_gate[l],
                         b_gate[l], w_out[l])
        x = layer_norm(ALPHA * x + m, ln2_g[l], ln2_b[l])
        x = layer_norm(ALPHA * x + 0.5 * swiglu_ffn(x, ffn2_w1[l], ffn2_w2[l]), ln3_g[l], ln3_b[l])
    return x
```

```python
import functools

import jax
import jax.numpy as jnp
from jax import lax
from jax.experimental import pallas as pl
from jax.experimental.pallas import tpu as pltpu

F32 = jnp.float32
BF16 = jnp.bfloat16

D_MODEL = 1024
DEPTH = 4
CHUNK = 64
BRANCH_W = 256
N_BRANCH = 4
POOL_WINDOWS = (2, 4, 8, 16)
POOL_GC = BRANCH_W // len(POOL_WINDOWS)
RET_HEADS = 4
RET_DK = BRANCH_W // RET_HEADS
ROPE_BASE = 10000.0
SGU_LEN = 128
SGU_GROUPS = 4
D_FF = 2816
ALPHA = (2.0 * DEPTH) ** 0.25
LN_EPS = 1e-5
GN_EPS = 1e-5
IN_COLS = 10 * BRANCH_W

COL_BG, COL_CG, COL_XIN, COL_POOL = 0, 256, 512, 768
COL_Q, COL_K, COL_V, COL_G = 1024, 1280, 1536, 1792
COL_SU, COL_SV = 2048, 2304

FFN_TM = 512
FFN_TF = 1408
MIX_TS = 512
CONV_HIST = 8
POOL_HIST = 16
VMEM_LIMIT = 56 * 1024 * 1024


def _dot(a, b):
    return jnp.dot(a, b, preferred_element_type=F32)


def _layer_norm_rows(y, g, b, eps):
    mu = jnp.mean(y, axis=-1, keepdims=True)
    d = y - mu
    var = jnp.mean(d * d, axis=-1, keepdims=True)
    return d * lax.rsqrt(var + eps) * g + b


def _ffn_kernel(x_ref, w1_ref, w2_ref, g_ref, b_ref, o_ref):
    x = x_ref[...]
    xb = x.astype(BF16)
    acc = None
    for c in range(D_FF // FFN_TF):
        lo = c * FFN_TF
        gate = _dot(xb, w1_ref[:, lo:lo + FFN_TF])
        up = _dot(xb, w1_ref[:, D_FF + lo:D_FF + lo + FFN_TF])
        act = (gate * jax.nn.sigmoid(gate) * up).astype(BF16)
        part = _dot(act, w2_ref[lo:lo + FFN_TF, :])
        acc = part if acc is None else acc + part
    y = ALPHA * x + 0.5 * acc
    o_ref[...] = _layer_norm_rows(y, g_ref[...], b_ref[...], LN_EPS)


def _ffn_layer(x2d, w1_all, w2_all, g_all, b_all, layer):
    m = x2d.shape[0]
    const = lambda i: (layer, 0, 0)
    single = pl.Buffered(1)
    return pl.pallas_call(
        _ffn_kernel,
        grid=(m // FFN_TM,),
        in_specs=[
            pl.BlockSpec((FFN_TM, D_MODEL), lambda i: (i, 0)),
            pl.BlockSpec((None, D_MODEL, 2 * D_FF), const, pipeline_mode=single),
            pl.BlockSpec((None, D_FF, D_MODEL), const, pipeline_mode=single),
            pl.BlockSpec((None, 1, D_MODEL), const, pipeline_mode=single),
            pl.BlockSpec((None, 1, D_MODEL), const, pipeline_mode=single),
        ],
        out_specs=pl.BlockSpec((FFN_TM, D_MODEL), lambda i: (i, 0)),
        out_shape=jax.ShapeDtypeStruct((m, D_MODEL), F32),
        compiler_params=pltpu.CompilerParams(
            dimension_semantics=("arbitrary",), vmem_limit_bytes=VMEM_LIMIT),
        name="ffn_ln",
    )(x2d, w1_all, w2_all, g_all, b_all)


def _lane_group_mask(shape, group):
    lane = lax.broadcasted_iota(jnp.int32, shape, 1)
    return (lane // RET_DK) == group


def _stack_masked(x):
    return jnp.concatenate(
        [jnp.where(_lane_group_mask(x.shape, g), x, 0.0) for g in range(RET_HEADS)], axis=0)


def _group_mean(x, avg_ref):
    hi = x.astype(BF16)
    lo = (x - hi.astype(F32)).astype(BF16)
    return _dot(hi, avg_ref[...]) + _dot(lo, avg_ref[...])


def _mixer_kernel(h_ref, w_in_ref, w_gate_ref, b_gate_ref, w_branch_ref, w_out_ref,
                  conv_w_ref, pool_bd_ref, pool_scale_ref, gn_g_ref, gn_b_ref,
                  sln_g_ref, sln_b_ref, sgu_w_ref, sgu_bias_ref,
                  cos_ref, sin_ref, dcat_ref, xi_ref, zeta_ref, cd_ref, bm_ref, avg_ref,
                  ln_g_ref, ln_b_ref,
                  o_ref,
                  z_ref, cext_ref, pext_ref, q_ref, k_ref, ret_ref, state_ref, merged_ref):
    ts = h_ref.shape[0]
    s = pl.program_id(1)

    @pl.when(s == 0)
    def _():
        cext_ref[0:CONV_HIST, :] = jnp.zeros((CONV_HIST, BRANCH_W), F32)
        pext_ref[0:POOL_HIST, :] = jnp.zeros((POOL_HIST, BRANCH_W), F32)
        state_ref[...] = jnp.zeros_like(state_ref)

    h = h_ref[...]
    hb = h.astype(BF16)
    z_ref[...] = _dot(hb, w_in_ref[...])

    c = z_ref[:, COL_CG:COL_CG + BRANCH_W] * z_ref[:, COL_XIN:COL_XIN + BRANCH_W]
    cext_ref[CONV_HIST:CONV_HIST + ts, :] = c
    cw = conv_w_ref[...]
    conv = (cw[0:1, :] * cext_ref[CONV_HIST - 2:CONV_HIST - 2 + ts, :]
            + cw[1:2, :] * cext_ref[CONV_HIST - 1:CONV_HIST - 1 + ts, :]
            + cw[2:3, :] * c)
    y_a = z_ref[:, COL_BG:COL_BG + BRANCH_W] * conv
    cext_ref[0:CONV_HIST, :] = c[ts - CONV_HIST:, :]
    merged_ref[...] = _branch_term(y_a, hb, 0, w_branch_ref, w_gate_ref, b_gate_ref)

    zp = z_ref[:, COL_POOL:COL_POOL + BRANCH_W]
    pext_ref[POOL_HIST:POOL_HIST + ts, :] = zp
    ext = pext_ref[...]
    s2 = ext + pltpu.roll(ext, 1, 0)
    s4 = s2 + pltpu.roll(s2, 2, 0)
    s8 = s4 + pltpu.roll(s4, 4, 0)
    s16 = s8 + pltpu.roll(s8, 8, 0)
    lane = lax.broadcasted_iota(jnp.int32, (ts, BRANCH_W), 1)
    grp = lane // POOL_GC
    wsum = jnp.where(grp == 0, s2[POOL_HIST:], jnp.where(
        grp == 1, s4[POOL_HIST:], jnp.where(grp == 2, s8[POOL_HIST:], s16[POOL_HIST:])))
    win = jnp.where(grp == 0, 2, jnp.where(grp == 1, 4, jnp.where(grp == 2, 8, 16)))
    t_glob = s * ts + lax.broadcasted_iota(jnp.int32, (ts, BRANCH_W), 0)
    cnt = jnp.minimum(t_glob + 1, win).astype(F32)
    mixed = wsum / cnt - zp
    y_p = _dot(mixed.astype(BF16), pool_bd_ref[...]) * pool_scale_ref[...]
    pext_ref[0:POOL_HIST, :] = zp[ts - POOL_HIST:, :]
    merged_ref[...] += _branch_term(y_p, hb, 1, w_branch_ref, w_gate_ref, b_gate_ref)

    cos = cos_ref[...]
    sin = sin_ref[...]
    first_half = (lane % RET_DK) < (RET_DK // 2)

    def rope(x):
        swapped = jnp.where(first_half, pltpu.roll(x, BRANCH_W - RET_DK // 2, 1),
                            pltpu.roll(x, RET_DK // 2, 1))
        return x * cos + swapped * sin

    q_ref[...] = rope(z_ref[:, COL_Q:COL_Q + BRANCH_W])
    k_ref[...] = rope(z_ref[:, COL_K:COL_K + BRANCH_W]) * (RET_DK ** -0.5)

    for n in range(ts // CHUNK):
        rows = slice(n * CHUNK, (n + 1) * CHUNK)
        qc = q_ref[rows, :]
        kc = k_ref[rows, :]
        vc = z_ref[rows, COL_V:COL_V + BRANCH_W]
        k_stack = _stack_masked(kc).astype(BF16)
        v_stack = _stack_masked(vc).astype(BF16)
        scores = lax.dot_general(qc.astype(BF16), k_stack, (((1,), (1,)), ((), ())),
                                 preferred_element_type=F32)
        probs = (scores * dcat_ref[...]).astype(BF16)
        inner = _dot(probs, v_stack)
        state = state_ref[...]
        cross = _dot((qc * xi_ref[...]).astype(BF16), state.astype(BF16))
        ret_ref[rows, :] = inner + cross
        kz_t = jnp.transpose(kc * zeta_ref[...]).astype(BF16)
        kv = _dot(kz_t, vc.astype(BF16))
        state_ref[...] = state * cd_ref[...] + kv * bm_ref[...]

    o = ret_ref[...]
    mu = _group_mean(o, avg_ref)
    d = o - mu
    var = _group_mean(d * d, avg_ref)
    o_n = d * lax.rsqrt(var + GN_EPS) * gn_g_ref[...] + gn_b_ref[...]
    gate_r = z_ref[:, COL_G:COL_G + BRANCH_W]
    y_r = gate_r * jax.nn.sigmoid(gate_r) * o_n
    merged_ref[...] += _branch_term(y_r, hb, 2, w_branch_ref, w_gate_ref, b_gate_ref)

    u = jax.nn.gelu(z_ref[:, COL_SU:COL_SU + BRANCH_W])
    v = jax.nn.gelu(z_ref[:, COL_SV:COL_SV + BRANCH_W])
    v = _layer_norm_rows(v, sln_g_ref[...], sln_b_ref[...], LN_EPS)
    wi = lax.broadcasted_iota(jnp.int32, (SGU_LEN, SGU_GROUPS * SGU_LEN), 0)
    wj = lax.broadcasted_iota(jnp.int32, (SGU_LEN, SGU_GROUPS * SGU_LEN), 1) % SGU_LEN
    w_s = jnp.where((wj // CHUNK) <= (wi // CHUNK), sgu_w_ref[...], 0.0).astype(BF16)
    pieces = []
    for n in range(ts // SGU_LEN):
        v_stack = _stack_masked(v[n * SGU_LEN:(n + 1) * SGU_LEN, :]).astype(BF16)
        pieces.append(_dot(w_s, v_stack) + sgu_bias_ref[...])
    y_s = u * jnp.concatenate(pieces, axis=0)
    merged = merged_ref[...] + _branch_term(y_s, hb, 3, w_branch_ref, w_gate_ref, b_gate_ref)

    out = _dot(merged.astype(BF16), w_out_ref[...])
    o_ref[...] = _layer_norm_rows(ALPHA * h + out, ln_g_ref[...], ln_b_ref[...], LN_EPS)


def _branch_term(y, hb, n, w_branch_ref, w_gate_ref, b_gate_ref):
    cols = slice(n * D_MODEL, (n + 1) * D_MODEL)
    gate = jax.nn.sigmoid(_dot(hb, w_gate_ref[:, cols]) + b_gate_ref[:, cols])
    return gate * _dot(y.astype(BF16), w_branch_ref[n])


def _retention_constants():
    heads = jnp.arange(RET_HEADS, dtype=F32)
    log_gamma = jnp.log1p(-(2.0 ** (-5.0 - heads)))
    idx = jnp.arange(CHUNK, dtype=F32)
    diff = idx[:, None] - idx[None, :]
    decay = jnp.where(diff >= 0,
                      jnp.exp(log_gamma[:, None, None] * jnp.maximum(diff, 0.0)), 0.0)
    dcat = jnp.transpose(decay, (1, 0, 2)).reshape(CHUNK, RET_HEADS * CHUNK)
    zeta = jnp.exp(log_gamma[:, None] * (CHUNK - 1 - idx)[None, :])
    xi = jnp.exp(log_gamma[:, None] * (idx + 1.0)[None, :])
    chunk_decay = jnp.exp(log_gamma * CHUNK)
    zeta_l = jnp.repeat(zeta.T, RET_DK, axis=1)
    xi_l = jnp.repeat(xi.T, RET_DK, axis=1)
    head_of = jnp.arange(BRANCH_W) // RET_DK
    same = head_of[:, None] == head_of[None, :]
    bm = same.astype(F32)
    cd = jnp.where(same, chunk_decay[head_of][:, None], 0.0)
    avg = (bm / RET_DK).astype(BF16)
    return dcat, xi_l, zeta_l, cd, bm, avg


def _rope_tables(seq):
    half = RET_DK // 2
    inv = ROPE_BASE ** (-jnp.arange(half, dtype=F32) / half)
    ang = jnp.arange(seq).astype(F32)[:, None] * inv[None, :]
    cos = jnp.cos(ang)
    sin = jnp.sin(ang)
    cos_l = jnp.tile(jnp.concatenate([cos, cos], axis=1), (1, RET_HEADS))
    sin_l = jnp.tile(jnp.concatenate([-sin, sin], axis=1), (1, RET_HEADS))
    return cos_l, sin_l


def _mixer_layer(x3d, p, consts, layer):
    bsz, seq, _ = x3d.shape
    ts = MIX_TS
    single = pl.Buffered(1)

    def lspec(shape):
        nd = len(shape)
        return pl.BlockSpec((None,) + shape, lambda b, s: (layer,) + (0,) * nd,
                            pipeline_mode=single)

    def cspec(shape):
        nd = len(shape)
        return pl.BlockSpec(shape, lambda b, s: (0,) * nd, pipeline_mode=single)

    cos_l, sin_l, dcat, xi_l, zeta_l, cd, bm, avg = consts
    in_specs = [
        pl.BlockSpec((None, ts, D_MODEL), lambda b, s: (b, s, 0)),
        lspec((D_MODEL, IN_COLS)), lspec((D_MODEL, N_BRANCH * D_MODEL)),
        lspec((1, N_BRANCH * D_MODEL)), lspec((N_BRANCH, BRANCH_W, D_MODEL)),
        lspec((D_MODEL, D_MODEL)),
        lspec((3, BRANCH_W)), lspec((BRANCH_W, BRANCH_W)), lspec((1, BRANCH_W)),
        lspec((1, BRANCH_W)), lspec((1, BRANCH_W)), lspec((1, BRANCH_W)), lspec((1, BRANCH_W)),
        lspec((SGU_LEN, SGU_GROUPS * SGU_LEN)), lspec((SGU_LEN, BRANCH_W)),
        pl.BlockSpec((ts, BRANCH_W), lambda b, s: (s, 0)),
        pl.BlockSpec((ts, BRANCH_W), lambda b, s: (s, 0)),
        cspec(dcat.shape), cspec(xi_l.shape), cspec(zeta_l.shape),
        cspec(cd.shape), cspec(bm.shape), cspec(avg.shape),
        lspec((1, D_MODEL)), lspec((1, D_MODEL)),
    ]
    scratch = [
        pltpu.VMEM((ts, IN_COLS), F32),
        pltpu.VMEM((CONV_HIST + ts, BRANCH_W), F32),
        pltpu.VMEM((POOL_HIST + ts, BRANCH_W), F32),
        pltpu.VMEM((ts, BRANCH_W), F32),
        pltpu.VMEM((ts, BRANCH_W), F32),
        pltpu.VMEM((ts, BRANCH_W), F32),
        pltpu.VMEM((BRANCH_W, BRANCH_W), F32),
        pltpu.VMEM((ts, D_MODEL), F32),
    ]
    return pl.pallas_call(
        _mixer_kernel,
        grid=(bsz, seq // ts),
        in_specs=in_specs,
        out_specs=pl.BlockSpec((None, ts, D_MODEL), lambda b, s: (b, s, 0)),
        out_shape=jax.ShapeDtypeStruct((bsz, seq, D_MODEL), F32),
        scratch_shapes=scratch,
        compiler_params=pltpu.CompilerParams(
            dimension_semantics=("arbitrary", "arbitrary"), vmem_limit_bytes=VMEM_LIMIT),
        name="mixer_ln",
    )(x3d, p["w_in"], p["w_gate"], p["b_gate"], p["w_branch"], p["w_out"],
      p["conv_w"], p["pool_bd"], p["pool_scale"], p["gn_g"], p["gn_b"],
      p["sln_g"], p["sln_b"], p["sgu_w"], p["sgu_bias"],
      cos_l, sin_l, dcat, xi_l, zeta_l, cd, bm, avg, p["ln_g"], p["ln_b"])


def _prep_mixer_params(w_in, conv_w, pool_w, pool_scale, ret_gn_g, ret_gn_b, sgu_ln_g, sgu_ln_b,
                       sgu_w, sgu_b, w_branch, w_gate, b_gate, w_out, ln2_g, ln2_b):
    depth = w_in.shape[0]
    groups = pool_w.shape[1]
    eye = jnp.eye(groups, dtype=pool_w.dtype)
    pool_bd = jnp.einsum("lgcd,gh->lgchd", pool_w, eye).reshape(depth, BRANCH_W, BRANCH_W)
    row = lambda a: a[:, None, :]
    return {
        "w_in": w_in.astype(BF16), "w_gate": w_gate.astype(BF16), "b_gate": row(b_gate),
        "w_branch": w_branch.astype(BF16), "w_out": w_out.astype(BF16),
        "conv_w": conv_w, "pool_bd": pool_bd.astype(BF16), "pool_scale": row(pool_scale),
        "gn_g": row(ret_gn_g), "gn_b": row(ret_gn_b),
        "sln_g": row(sgu_ln_g), "sln_b": row(sgu_ln_b),
        "sgu_w": jnp.transpose(sgu_w, (0, 2, 1, 3)).reshape(depth, SGU_LEN, SGU_GROUPS * SGU_LEN),
        "sgu_bias": jnp.repeat(jnp.transpose(sgu_b, (0, 2, 1)), BRANCH_W // SGU_GROUPS, axis=2),
        "ln_g": row(ln2_g), "ln_b": row(ln2_b),
    }


def kernel(x, ffn1_w1, ffn1_w2, ln1_g, ln1_b, w_in, conv_w, pool_w, pool_scale, ret_gn_g, ret_gn_b,
           sgu_ln_g, sgu_ln_b, sgu_w, sgu_b, w_branch, w_gate, b_gate, w_out, ln2_g, ln2_b,
           ffn2_w1, ffn2_w2, ln3_g, ln3_b):
    bsz, seq, d = x.shape
    depth = ffn1_w1.shape[0]
    row = lambda a: a[:, None, :]
    f1w1, f1w2 = ffn1_w1.astype(BF16), ffn1_w2.astype(BF16)
    f2w1, f2w2 = ffn2_w1.astype(BF16), ffn2_w2.astype(BF16)
    mp = _prep_mixer_params(w_in, conv_w, pool_w, pool_scale, ret_gn_g, ret_gn_b, sgu_ln_g,
                            sgu_ln_b, sgu_w, sgu_b, w_branch, w_gate, b_gate, w_out, ln2_g, ln2_b)
    consts = _rope_tables(seq) + _retention_constants()
    for l in range(depth):
        x2 = _ffn_layer(x.reshape(bsz * seq, d), f1w1, f1w2, row(ln1_g), row(ln1_b), l)
        x3 = _mixer_layer(x2.reshape(bsz, seq, d), mp, consts, l)
        x2 = _ffn_layer(x3.reshape(bsz * seq, d), f2w1, f2w2, row(ln3_g), row(ln3_b), l)
        x = x2.reshape(bsz, seq, d)
    return x
```

```python
import functools

import jax
import jax.numpy as jnp
from jax import lax
from jax.experimental import pallas as pl
from jax.experimental.pallas import tpu as pltpu

F32 = jnp.float32
BF16 = jnp.bfloat16

D_MODEL = 1024
DEPTH = 4
CHUNK = 64
BRANCH_W = 256
N_BRANCH = 4
POOL_WINDOWS = (2, 4, 8, 16)
POOL_GC = BRANCH_W // len(POOL_WINDOWS)
RET_HEADS = 4
RET_DK = BRANCH_W // RET_HEADS
ROPE_BASE = 10000.0
SGU_LEN = 128
SGU_GROUPS = 4
D_FF = 2816
ALPHA = (2.0 * DEPTH) ** 0.25
LN_EPS = 1e-5
GN_EPS = 1e-5
IN_COLS = 10 * BRANCH_W

COL_BG, COL_CG, COL_XIN, COL_POOL = 0, 256, 512, 768
COL_Q, COL_K, COL_V, COL_G = 1024, 1280, 1536, 1792
COL_SU, COL_SV = 2048, 2304

FFN_SUBS = (256, 256, 256, 256)
FFN_TM = sum(FFN_SUBS)
MIX_TS = 512
RET_CC = 128
CONV_HIST = 8
POOL_HIST = 16
VMEM_LIMIT = 56 * 1024 * 1024


def _dot(a, b):
    return jnp.dot(a, b, preferred_element_type=F32)


def _layer_norm_rows(y, g, b, eps):
    mu = jnp.mean(y, axis=-1, keepdims=True)
    d = y - mu
    var = jnp.mean(d * d, axis=-1, keepdims=True)
    return d * lax.rsqrt(var + eps) * g + b


def _ffn_kernel(x_ref, w1_ref, w2_ref, g_ref, b_ref, o_ref):
    lo = 0
    for rows_n in FFN_SUBS:
        rows = slice(lo, lo + rows_n)
        lo += rows_n
        x = x_ref[rows, :]
        xb = x.astype(BF16)
        gate = _dot(xb, w1_ref[:, 0:D_FF])
        up = _dot(xb, w1_ref[:, D_FF:2 * D_FF])
        act = (gate * jax.nn.sigmoid(gate) * up).astype(BF16)
        y = ALPHA * x + 0.5 * _dot(act, w2_ref[...])
        o_ref[rows, :] = _layer_norm_rows(y, g_ref[...], b_ref[...], LN_EPS)


def _ffn_layer(x2d, w1_all, w2_all, g_all, b_all, layer):
    m = x2d.shape[0]
    const = lambda i: (layer, 0, 0)
    single = pl.Buffered(1)
    return pl.pallas_call(
        _ffn_kernel,
        grid=(m // FFN_TM,),
        in_specs=[
            pl.BlockSpec((FFN_TM, D_MODEL), lambda i: (i, 0)),
            pl.BlockSpec((None, D_MODEL, 2 * D_FF), const, pipeline_mode=single),
            pl.BlockSpec((None, D_FF, D_MODEL), const, pipeline_mode=single),
            pl.BlockSpec((None, 1, D_MODEL), const, pipeline_mode=single),
            pl.BlockSpec((None, 1, D_MODEL), const, pipeline_mode=single),
        ],
        out_specs=pl.BlockSpec((FFN_TM, D_MODEL), lambda i: (i, 0)),
        out_shape=jax.ShapeDtypeStruct((m, D_MODEL), F32),
        compiler_params=pltpu.CompilerParams(
            dimension_semantics=("arbitrary",), vmem_limit_bytes=VMEM_LIMIT),
        name="ffn_ln",
    )(x2d, w1_all, w2_all, g_all, b_all)


def _lane_group_mask(shape, group):
    lane = lax.broadcasted_iota(jnp.int32, shape, 1)
    return (lane // RET_DK) == group


def _stack_masked(x):
    return jnp.concatenate(
        [jnp.where(_lane_group_mask(x.shape, g), x, 0.0) for g in range(RET_HEADS)], axis=0)


def _group_mean(x, avg_ref):
    hi = x.astype(BF16)
    lo = (x - hi.astype(F32)).astype(BF16)
    return _dot(hi, avg_ref[...]) + _dot(lo, avg_ref[...])


def _branch_term(y, hb, n, w_branch_ref, w_gate_ref, b_gate_ref):
    cols = slice(n * D_MODEL, (n + 1) * D_MODEL)
    gate = jax.nn.sigmoid(_dot(hb, w_gate_ref[:, cols]) + b_gate_ref[:, cols])
    return gate * _dot(y.astype(BF16), w_branch_ref[n])


def _mixer_kernel(nseq,
                  h_ref, w_in_ref, w_gate_ref, b_gate_ref, w_branch_ref, w_out_ref,
                  conv_w_ref, pool_bd_ref, pool_scale_ref, gn_g_ref, gn_b_ref,
                  sln_g_ref, sln_b_ref, sgu_w_ref, sgu_bias_ref,
                  cos_ref, sin_ref, dcat_ref, xi_ref, zeta_ref, cd_ref, bm_ref, avg_ref,
                  ln_g_ref, ln_b_ref,
                  o_ref,
                  z_ref, cext_ref, pext_ref, q_ref, k_ref, ret_ref, state_ref, merged_ref,
                  mprev_ref, hres_ref):
    ts = h_ref.shape[0]
    i = pl.program_id(0)
    last = pl.num_programs(0) - 1
    s = i % nseq

    def finish_previous_tile():
        out = _dot(mprev_ref[...], w_out_ref[...])
        o_ref[...] = _layer_norm_rows(hres_ref[...] + out, ln_g_ref[...], ln_b_ref[...], LN_EPS)

    @pl.when(i == 0)
    def _():
        mprev_ref[...] = jnp.zeros_like(mprev_ref)
        hres_ref[...] = jnp.zeros_like(hres_ref)

    @pl.when(s == 0)
    def _():
        cext_ref[0:CONV_HIST, :] = jnp.zeros((CONV_HIST, BRANCH_W), F32)
        pext_ref[0:POOL_HIST, :] = jnp.zeros((POOL_HIST, BRANCH_W), F32)
        state_ref[...] = jnp.zeros_like(state_ref)

    @pl.when(i == last)
    def _():
        finish_previous_tile()

    @pl.when(i < last)
    def _():
        finish_previous_tile()

        h = h_ref[...]
        hb = h.astype(BF16)
        z_ref[...] = _dot(hb, w_in_ref[...])

        c = z_ref[:, COL_CG:COL_CG + BRANCH_W] * z_ref[:, COL_XIN:COL_XIN + BRANCH_W]
        cext_ref[CONV_HIST:CONV_HIST + ts, :] = c
        cw = conv_w_ref[...]
        conv = (cw[0:1, :] * cext_ref[CONV_HIST - 2:CONV_HIST - 2 + ts, :]
                + cw[1:2, :] * cext_ref[CONV_HIST - 1:CONV_HIST - 1 + ts, :]
                + cw[2:3, :] * c)
        y_a = z_ref[:, COL_BG:COL_BG + BRANCH_W] * conv
        cext_ref[0:CONV_HIST, :] = c[ts - CONV_HIST:, :]
        merged_ref[...] = _branch_term(y_a, hb, 0, w_branch_ref, w_gate_ref, b_gate_ref)

        zp = z_ref[:, COL_POOL:COL_POOL + BRANCH_W]
        pext_ref[POOL_HIST:POOL_HIST + ts, :] = zp
        ext = pext_ref[...]
        s2 = ext + pltpu.roll(ext, 1, 0)
        s4 = s2 + pltpu.roll(s2, 2, 0)
        s8 = s4 + pltpu.roll(s4, 4, 0)
        s16 = s8 + pltpu.roll(s8, 8, 0)
        lane = lax.broadcasted_iota(jnp.int32, (ts, BRANCH_W), 1)
        grp = lane // POOL_GC
        wsum = jnp.where(grp == 0, s2[POOL_HIST:], jnp.where(
            grp == 1, s4[POOL_HIST:], jnp.where(grp == 2, s8[POOL_HIST:], s16[POOL_HIST:])))
        win = jnp.where(grp == 0, 2, jnp.where(grp == 1, 4, jnp.where(grp == 2, 8, 16)))
        t_glob = s * ts + lax.broadcasted_iota(jnp.int32, (ts, BRANCH_W), 0)
        cnt = jnp.minimum(t_glob + 1, win).astype(F32)
        mixed = wsum / cnt - zp
        y_p = _dot(mixed.astype(BF16), pool_bd_ref[...]) * pool_scale_ref[...]
        pext_ref[0:POOL_HIST, :] = zp[ts - POOL_HIST:, :]
        merged_ref[...] += _branch_term(y_p, hb, 1, w_branch_ref, w_gate_ref, b_gate_ref)

        cos = cos_ref[...]
        sin = sin_ref[...]
        first_half = (lane % RET_DK) < (RET_DK // 2)

        def rope(x):
            swapped = jnp.where(first_half, pltpu.roll(x, BRANCH_W - RET_DK // 2, 1),
                                pltpu.roll(x, RET_DK // 2, 1))
            return x * cos + swapped * sin

        q_ref[...] = rope(z_ref[:, COL_Q:COL_Q + BRANCH_W])
        k_ref[...] = rope(z_ref[:, COL_K:COL_K + BRANCH_W]) * (RET_DK ** -0.5)

        state = state_ref[...]
        cd = cd_ref[...]
        bm = bm_ref[...]
        for n in range(ts // RET_CC):
            rows = slice(n * RET_CC, (n + 1) * RET_CC)
            qc = q_ref[rows, :]
            kc = k_ref[rows, :]
            vc = z_ref[rows, COL_V:COL_V + BRANCH_W]
            k_stack = _stack_masked(kc).astype(BF16)
            v_stack = _stack_masked(vc).astype(BF16)
            scores = lax.dot_general(qc.astype(BF16), k_stack, (((1,), (1,)), ((), ())),
                                     preferred_element_type=F32)
            probs = (scores * dcat_ref[...]).astype(BF16)
            inner = _dot(probs, v_stack)
            cross = _dot((qc * xi_ref[...]).astype(BF16), state.astype(BF16))
            ret_ref[rows, :] = inner + cross
            kz_t = jnp.transpose(kc * zeta_ref[...]).astype(BF16)
            kv = _dot(kz_t, vc.astype(BF16))
            state = state * cd + kv * bm
        state_ref[...] = state

        o = ret_ref[...]
        mu = _group_mean(o, avg_ref)
        d = o - mu
        var = _group_mean(d * d, avg_ref)
        o_n = d * lax.rsqrt(var + GN_EPS) * gn_g_ref[...] + gn_b_ref[...]
        gate_r = z_ref[:, COL_G:COL_G + BRANCH_W]
        y_r = gate_r * jax.nn.sigmoid(gate_r) * o_n
        merged_ref[...] += _branch_term(y_r, hb, 2, w_branch_ref, w_gate_ref, b_gate_ref)

        u = jax.nn.gelu(z_ref[:, COL_SU:COL_SU + BRANCH_W])
        v = jax.nn.gelu(z_ref[:, COL_SV:COL_SV + BRANCH_W])
        v = _layer_norm_rows(v, sln_g_ref[...], sln_b_ref[...], LN_EPS)
        wi = lax.broadcasted_iota(jnp.int32, (SGU_LEN, SGU_GROUPS * SGU_LEN), 0)
        wj = lax.broadcasted_iota(jnp.int32, (SGU_LEN, SGU_GROUPS * SGU_LEN), 1) % SGU_LEN
        w_s = jnp.where((wj // CHUNK) <= (wi // CHUNK), sgu_w_ref[...], 0.0).astype(BF16)
        pieces = []
        for n in range(ts // SGU_LEN):
            v_stack = _stack_masked(v[n * SGU_LEN:(n + 1) * SGU_LEN, :]).astype(BF16)
            pieces.append(_dot(w_s, v_stack) + sgu_bias_ref[...])
        y_s = u * jnp.concatenate(pieces, axis=0)
        merged = merged_ref[...] + _branch_term(y_s, hb, 3, w_branch_ref, w_gate_ref, b_gate_ref)

        mprev_ref[...] = merged.astype(BF16)
        hres_ref[...] = ALPHA * h


def _retention_constants():
    cc = RET_CC
    heads = jnp.arange(RET_HEADS, dtype=F32)
    log_gamma = jnp.log1p(-(2.0 ** (-5.0 - heads)))
    idx = jnp.arange(cc, dtype=F32)
    diff = idx[:, None] - idx[None, :]
    decay = jnp.where(diff >= 0,
                      jnp.exp(log_gamma[:, None, None] * jnp.maximum(diff, 0.0)), 0.0)
    dcat = jnp.transpose(decay, (1, 0, 2)).reshape(cc, RET_HEADS * cc)
    zeta = jnp.exp(log_gamma[:, None] * (cc - 1 - idx)[None, :])
    xi = jnp.exp(log_gamma[:, None] * (idx + 1.0)[None, :])
    chunk_decay = jnp.exp(log_gamma * cc)
    zeta_l = jnp.repeat(zeta.T, RET_DK, axis=1)
    xi_l = jnp.repeat(xi.T, RET_DK, axis=1)
    head_of = jnp.arange(BRANCH_W) // RET_DK
    same = head_of[:, None] == head_of[None, :]
    bm = same.astype(F32)
    cd = jnp.where(same, chunk_decay[head_of][:, None], 0.0)
    avg = (bm / RET_DK).astype(BF16)
    return dcat, xi_l, zeta_l, cd, bm, avg


def _rope_tables(seq):
    half = RET_DK // 2
    inv = ROPE_BASE ** (-jnp.arange(half, dtype=F32) / half)
    ang = jnp.arange(seq).astype(F32)[:, None] * inv[None, :]
    cos = jnp.cos(ang)
    sin = jnp.sin(ang)
    cos_l = jnp.tile(jnp.concatenate([cos, cos], axis=1), (1, RET_HEADS))
    sin_l = jnp.tile(jnp.concatenate([-sin, sin], axis=1), (1, RET_HEADS))
    return cos_l, sin_l


def _mixer_layer(x3d, p, consts, layer):
    bsz, seq, _ = x3d.shape
    ts = MIX_TS
    nseq = seq // ts
    n = bsz * nseq
    single = pl.Buffered(1)

    def lspec(shape):
        nd = len(shape)
        return pl.BlockSpec((None,) + shape, lambda i: (layer,) + (0,) * nd, pipeline_mode=single)

    def cspec(shape):
        nd = len(shape)
        return pl.BlockSpec(shape, lambda i: (0,) * nd, pipeline_mode=single)

    def tile_in(i):
        j = jnp.minimum(i, n - 1)
        return (j // nseq, j % nseq, 0)

    def tile_out(i):
        j = jnp.maximum(i - 1, 0)
        return (j // nseq, j % nseq, 0)

    table = pl.BlockSpec((ts, BRANCH_W), lambda i: (jnp.minimum(i, n - 1) % nseq, 0))
    cos_l, sin_l, dcat, xi_l, zeta_l, cd, bm, avg = consts
    in_specs = [
        pl.BlockSpec((None, ts, D_MODEL), tile_in),
        lspec((D_MODEL, IN_COLS)), lspec((D_MODEL, N_BRANCH * D_MODEL)),
        lspec((1, N_BRANCH * D_MODEL)), lspec((N_BRANCH, BRANCH_W, D_MODEL)),
        lspec((D_MODEL, D_MODEL)),
        lspec((3, BRANCH_W)), lspec((BRANCH_W, BRANCH_W)), lspec((1, BRANCH_W)),
        lspec((1, BRANCH_W)), lspec((1, BRANCH_W)), lspec((1, BRANCH_W)), lspec((1, BRANCH_W)),
        lspec((SGU_LEN, SGU_GROUPS * SGU_LEN)), lspec((SGU_LEN, BRANCH_W)),
        table, table,
        cspec(dcat.shape), cspec(xi_l.shape), cspec(zeta_l.shape),
        cspec(cd.shape), cspec(bm.shape), cspec(avg.shape),
        lspec((1, D_MODEL)), lspec((1, D_MODEL)),
    ]
    scratch = [
        pltpu.VMEM((ts, IN_COLS), F32),
        pltpu.VMEM((CONV_HIST + ts, BRANCH_W), F32),
        pltpu.VMEM((POOL_HIST + ts, BRANCH_W), F32),
        pltpu.VMEM((ts, BRANCH_W), F32),
        pltpu.VMEM((ts, BRANCH_W), F32),
        pltpu.VMEM((ts, BRANCH_W), F32),
        pltpu.VMEM((BRANCH_W, BRANCH_W), F32),
        pltpu.VMEM((ts, D_MODEL), F32),
        pltpu.VMEM((ts, D_MODEL), BF16),
        pltpu.VMEM((ts, D_MODEL), F32),
    ]
    return pl.pallas_call(
        functools.partial(_mixer_kernel, nseq),
        grid=(n + 1,),
        in_specs=in_specs,
        out_specs=pl.BlockSpec((None, ts, D_MODEL), tile_out),
        out_shape=jax.ShapeDtypeStruct((bsz, seq, D_MODEL), F32),
        scratch_shapes=scratch,
        compiler_params=pltpu.CompilerParams(
            dimension_semantics=("arbitrary",), vmem_limit_bytes=VMEM_LIMIT),
        name="mixer_ln",
    )(x3d, p["w_in"], p["w_gate"], p["b_gate"], p["w_branch"], p["w_out"],
      p["conv_w"], p["pool_bd"], p["pool_scale"], p["gn_g"], p["gn_b"],
      p["sln_g"], p["sln_b"], p["sgu_w"], p["sgu_bias"],
      cos_l, sin_l, dcat, xi_l, zeta_l, cd, bm, avg, p["ln_g"], p["ln_b"])


def _prep_mixer_params(w_in, conv_w, pool_w, pool_scale, ret_gn_g, ret_gn_b, sgu_ln_g, sgu_ln_b,
                       sgu_w, sgu_b, w_branch, w_gate, b_gate, w_out, ln2_g, ln2_b):
    depth = w_in.shape[0]
    groups = pool_w.shape[1]
    eye = jnp.eye(groups, dtype=pool_w.dtype)
    pool_bd = jnp.einsum("lgcd,gh->lgchd", pool_w, eye).reshape(depth, BRANCH_W, BRANCH_W)
    row = lambda a: a[:, None, :]
    return {
        "w_in": w_in.astype(BF16), "w_gate": w_gate.astype(BF16), "b_gate": row(b_gate),
        "w_branch": w_branch.astype(BF16), "w_out": w_out.astype(BF16),
        "conv_w": conv_w, "pool_bd": pool_bd.astype(BF16), "pool_scale": row(pool_scale),
        "gn_g": row(ret_gn_g), "gn_b": row(ret_gn_b),
        "sln_g": row(sgu_ln_g), "sln_b": row(sgu_ln_b),
        "sgu_w": jnp.transpose(sgu_w, (0, 2, 1, 3)).reshape(depth, SGU_LEN, SGU_GROUPS * SGU_LEN),
        "sgu_bias": jnp.repeat(jnp.transpose(sgu_b, (0, 2, 1)), BRANCH_W // SGU_GROUPS, axis=2),
        "ln_g": row(ln2_g), "ln_b": row(ln2_b),
    }


def kernel(x, ffn1_w1, ffn1_w2, ln1_g, ln1_b, w_in, conv_w, pool_w, pool_scale, ret_gn_g, ret_gn_b,
           sgu_ln_g, sgu_ln_b, sgu_w, sgu_b, w_branch, w_gate, b_gate, w_out, ln2_g, ln2_b,
           ffn2_w1, ffn2_w2, ln3_g, ln3_b):
    bsz, seq, d = x.shape
    depth = ffn1_w1.shape[0]
    row = lambda a: a[:, None, :]
    f1w1, f1w2 = ffn1_w1.astype(BF16), ffn1_w2.astype(BF16)
    f2w1, f2w2 = ffn2_w1.astype(BF16), ffn2_w2.astype(BF16)
    mp = _prep_mixer_params(w_in, conv_w, pool_w, pool_scale, ret_gn_g, ret_gn_b, sgu_ln_g,
                            sgu_ln_b, sgu_w, sgu_b, w_branch, w_gate, b_gate, w_out, ln2_g, ln2_b)
    consts = _rope_tables(seq) + _retention_constants()
    for l in range(depth):
        x2 = _ffn_layer(x.reshape(bsz * seq, d), f1w1, f1w2, row(ln1_g), row(ln1_b), l)
        x3 = _mixer_layer(x2.reshape(bsz, seq, d), mp, consts, l)
        x2 = _ffn_layer(x3.reshape(bsz * seq, d), f2w1, f2w2, row(ln3_g), row(ln3_b), l)
        x = x2.reshape(bsz, seq, d)
    return x
```

```python
import functools
import math

import jax
import jax.numpy as jnp
from jax import lax
from jax.experimental import pallas as pl
from jax.experimental.pallas import tpu as pltpu

F32 = jnp.float32
BF16 = jnp.bfloat16

D_MODEL = 1024
DEPTH = 4
CHUNK = 64
BRANCH_W = 256
N_BRANCH = 4
POOL_WINDOWS = (2, 4, 8, 16)
POOL_GC = BRANCH_W // len(POOL_WINDOWS)
RET_HEADS = 4
RET_DK = BRANCH_W // RET_HEADS
ROPE_BASE = 10000.0
SGU_LEN = 128
SGU_GROUPS = 4
D_FF = 2816
ALPHA = (2.0 * DEPTH) ** 0.25
LN_EPS = 1e-5
GN_EPS = 1e-5
IN_COLS = 10 * BRANCH_W

COL_BG, COL_CG, COL_XIN, COL_POOL = 0, 256, 512, 768
COL_Q, COL_K, COL_V, COL_G = 1024, 1280, 1536, 1792
COL_SU, COL_SV = 2048, 2304

FFN_SUBS = (256, 256, 256, 256)
FFN_TM = sum(FFN_SUBS)
MIX_TS = 512
RET_CC = 128
CONV_HIST = 8
POOL_HIST = 16
CAST_BLOCKS = 16
VMEM_LIMIT = 56 * 1024 * 1024


def _dot(a, b):
    return jnp.dot(a, b, preferred_element_type=F32)


def _layer_norm_rows(y, g, b, eps):
    mu = jnp.mean(y, axis=-1, keepdims=True)
    d = y - mu
    var = jnp.mean(d * d, axis=-1, keepdims=True)
    return d * lax.rsqrt(var + eps) * g + b


def _cast_io(stack, layer, steps, step_of):
    _, rows, cols = stack.shape
    blocks = math.gcd(steps, CAST_BLOCKS)
    slab = rows // blocks
    assert slab * blocks == rows and slab % 16 == 0
    per = steps // blocks
    in_spec = pl.BlockSpec((None, slab, cols), lambda *g: (layer, step_of(*g) // per, 0))
    out_spec = pl.BlockSpec((slab, cols), lambda *g: (step_of(*g) // per, 0))
    return in_spec, out_spec, jax.ShapeDtypeStruct((rows, cols), BF16)


def _convert_slabs(src_refs, dst_refs):
    for s_ref, d_ref in zip(src_refs, dst_refs, strict=True):
        d_ref[...] = s_ref[...].astype(BF16)


def _ffn_kernel(n_cast, x_ref, w1_ref, w2_ref, g_ref, b_ref, *refs):
    cast_src, o_ref, cast_dst = refs[:n_cast], refs[n_cast], refs[n_cast + 1:]
    _convert_slabs(cast_src, cast_dst)
    lo = 0
    for rows_n in FFN_SUBS:
        rows = slice(lo, lo + rows_n)
        lo += rows_n
        x = x_ref[rows, :]
        xb = x.astype(BF16)
        gate = _dot(xb, w1_ref[:, 0:D_FF])
        up = _dot(xb, w1_ref[:, D_FF:2 * D_FF])
        act = (gate * jax.nn.sigmoid(gate) * up).astype(BF16)
        y = ALPHA * x + 0.5 * _dot(act, w2_ref[...])
        o_ref[rows, :] = _layer_norm_rows(y, g_ref[...], b_ref[...], LN_EPS)


def _ffn_layer(x2d, w1, w2, g_all, b_all, layer, cast_stacks, cast_layer):
    m = x2d.shape[0]
    steps = m // FFN_TM
    whole = lambda i: (0, 0)
    per_layer = lambda i: (layer, 0, 0)
    single = pl.Buffered(1)
    casts = [_cast_io(s, cast_layer, steps, lambda i: i) for s in cast_stacks]
    outs = pl.pallas_call(
        functools.partial(_ffn_kernel, len(casts)),
        grid=(steps,),
        in_specs=[
            pl.BlockSpec((FFN_TM, D_MODEL), lambda i: (i, 0)),
            pl.BlockSpec((D_MODEL, 2 * D_FF), whole, pipeline_mode=single),
            pl.BlockSpec((D_FF, D_MODEL), whole, pipeline_mode=single),
            pl.BlockSpec((None, 1, D_MODEL), per_layer, pipeline_mode=single),
            pl.BlockSpec((None, 1, D_MODEL), per_layer, pipeline_mode=single),
        ] + [c[0] for c in casts],
        out_specs=[pl.BlockSpec((FFN_TM, D_MODEL), lambda i: (i, 0))] + [c[1] for c in casts],
        out_shape=[jax.ShapeDtypeStruct((m, D_MODEL), F32)] + [c[2] for c in casts],
        compiler_params=pltpu.CompilerParams(
            dimension_semantics=("arbitrary",), vmem_limit_bytes=VMEM_LIMIT),
        name="ffn_ln",
    )(x2d, w1, w2, g_all, b_all, *cast_stacks)
    return outs[0], outs[1:]


def _lane_group_mask(shape, group):
    lane = lax.broadcasted_iota(jnp.int32, shape, 1)
    return (lane // RET_DK) == group


def _stack_masked(x):
    return jnp.concatenate(
        [jnp.where(_lane_group_mask(x.shape, g), x, 0.0) for g in range(RET_HEADS)], axis=0)


def _group_mean(x, avg_ref):
    hi = x.astype(BF16)
    lo = (x - hi.astype(F32)).astype(BF16)
    return _dot(hi, avg_ref[...]) + _dot(lo, avg_ref[...])


def _branch_term(y, hb, n, w_branch_ref, w_gate_ref, b_gate_ref):
    cols = slice(n * D_MODEL, (n + 1) * D_MODEL)
    gate = jax.nn.sigmoid(_dot(hb, w_gate_ref[:, cols]) + b_gate_ref[:, cols])
    return gate * _dot(y.astype(BF16), w_branch_ref[n * BRANCH_W:(n + 1) * BRANCH_W, :])


def _mixer_kernel(n_cast,
                  h_ref, w_in_ref, w_gate_ref, b_gate_ref, w_branch_ref, w_out_ref,
                  conv_w_ref, pool_bd_ref, pool_scale_ref, gn_g_ref, gn_b_ref,
                  sln_g_ref, sln_b_ref, sgu_w_ref, sgu_bias_ref,
                  cos_ref, sin_ref, dcat_ref, xi_ref, zeta_ref, cd_ref, bm_ref, avg_ref,
                  ln_g_ref, ln_b_ref, *refs):
    cast_src, o_ref, cast_dst = refs[:n_cast], refs[n_cast], refs[n_cast + 1:2 * n_cast + 1]
    (z_ref, cext_ref, pext_ref, q_ref, k_ref, ret_ref, state_ref,
     merged_ref) = refs[2 * n_cast + 1:]
    ts = h_ref.shape[0]
    s = pl.program_id(1)

    _convert_slabs(cast_src, cast_dst)

    @pl.when(s == 0)
    def _():
        cext_ref[0:CONV_HIST, :] = jnp.zeros((CONV_HIST, BRANCH_W), F32)
        pext_ref[0:POOL_HIST, :] = jnp.zeros((POOL_HIST, BRANCH_W), F32)
        state_ref[...] = jnp.zeros_like(state_ref)

    h = h_ref[...]
    hb = h.astype(BF16)
    z_ref[...] = _dot(hb, w_in_ref[...])

    c = z_ref[:, COL_CG:COL_CG + BRANCH_W] * z_ref[:, COL_XIN:COL_XIN + BRANCH_W]
    cext_ref[CONV_HIST:CONV_HIST + ts, :] = c
    cw = conv_w_ref[...]
    conv = (cw[0:1, :] * cext_ref[CONV_HIST - 2:CONV_HIST - 2 + ts, :]
            + cw[1:2, :] * cext_ref[CONV_HIST - 1:CONV_HIST - 1 + ts, :]
            + cw[2:3, :] * c)
    y_a = z_ref[:, COL_BG:COL_BG + BRANCH_W] * conv
    cext_ref[0:CONV_HIST, :] = c[ts - CONV_HIST:, :]
    merged_ref[...] = _branch_term(y_a, hb, 0, w_branch_ref, w_gate_ref, b_gate_ref)

    zp = z_ref[:, COL_POOL:COL_POOL + BRANCH_W]
    pext_ref[POOL_HIST:POOL_HIST + ts, :] = zp
    ext = pext_ref[...]
    s2 = ext + pltpu.roll(ext, 1, 0)
    s4 = s2 + pltpu.roll(s2, 2, 0)
    s8 = s4 + pltpu.roll(s4, 4, 0)
    s16 = s8 + pltpu.roll(s8, 8, 0)
    lane = lax.broadcasted_iota(jnp.int32, (ts, BRANCH_W), 1)
    grp = lane // POOL_GC
    wsum = jnp.where(grp == 0, s2[POOL_HIST:], jnp.where(
        grp == 1, s4[POOL_HIST:], jnp.where(grp == 2, s8[POOL_HIST:], s16[POOL_HIST:])))
    win = jnp.where(grp == 0, 2, jnp.where(grp == 1, 4, jnp.where(grp == 2, 8, 16)))
    t_glob = s * ts + lax.broadcasted_iota(jnp.int32, (ts, BRANCH_W), 0)
    cnt = jnp.minimum(t_glob + 1, win).astype(F32)
    mixed = wsum / cnt - zp
    y_p = _dot(mixed.astype(BF16), pool_bd_ref[...]) * pool_scale_ref[...]
    pext_ref[0:POOL_HIST, :] = zp[ts - POOL_HIST:, :]
    merged_ref[...] += _branch_term(y_p, hb, 1, w_branch_ref, w_gate_ref, b_gate_ref)

    cos = cos_ref[...]
    sin = sin_ref[...]
    first_half = (lane % RET_DK) < (RET_DK // 2)

    def rope(x):
        swapped = jnp.where(first_half, pltpu.roll(x, BRANCH_W - RET_DK // 2, 1),
                            pltpu.roll(x, RET_DK // 2, 1))
        return x * cos + swapped * sin

    q_ref[...] = rope(z_ref[:, COL_Q:COL_Q + BRANCH_W])
    k_ref[...] = rope(z_ref[:, COL_K:COL_K + BRANCH_W]) * (RET_DK ** -0.5)

    state = state_ref[...]
    cd = cd_ref[...]
    bm = bm_ref[...]
    for n in range(ts // RET_CC):
        rows = slice(n * RET_CC, (n + 1) * RET_CC)
        qc = q_ref[rows, :]
        kc = k_ref[rows, :]
        vc = z_ref[rows, COL_V:COL_V + BRANCH_W]
        k_stack = _stack_masked(kc).astype(BF16)
        v_stack = _stack_masked(vc).astype(BF16)
        scores = lax.dot_general(qc.astype(BF16), k_stack, (((1,), (1,)), ((), ())),
                                 preferred_element_type=F32)
        probs = (scores * dcat_ref[...]).astype(BF16)
        inner = _dot(probs, v_stack)
        cross = _dot((qc * xi_ref[...]).astype(BF16), state.astype(BF16))
        ret_ref[rows, :] = inner + cross
        kz_t = jnp.transpose(kc * zeta_ref[...]).astype(BF16)
        kv = _dot(kz_t, vc.astype(BF16))
        state = state * cd + kv * bm
    state_ref[...] = state

    o = ret_ref[...]
    mu = _group_mean(o, avg_ref)
    d = o - mu
    var = _group_mean(d * d, avg_ref)
    o_n = d * lax.rsqrt(var + GN_EPS) * gn_g_ref[...] + gn_b_ref[...]
    gate_r = z_ref[:, COL_G:COL_G + BRANCH_W]
    y_r = gate_r * jax.nn.sigmoid(gate_r) * o_n
    merged_ref[...] += _branch_term(y_r, hb, 2, w_branch_ref, w_gate_ref, b_gate_ref)

    u = jax.nn.gelu(z_ref[:, COL_SU:COL_SU + BRANCH_W])
    v = jax.nn.gelu(z_ref[:, COL_SV:COL_SV + BRANCH_W])
    v = _layer_norm_rows(v, sln_g_ref[...], sln_b_ref[...], LN_EPS)
    wi = lax.broadcasted_iota(jnp.int32, (SGU_LEN, SGU_GROUPS * SGU_LEN), 0)
    wj = lax.broadcasted_iota(jnp.int32, (SGU_LEN, SGU_GROUPS * SGU_LEN), 1) % SGU_LEN
    w_s = jnp.where((wj // CHUNK) <= (wi // CHUNK), sgu_w_ref[...], 0.0).astype(BF16)
    pieces = []
    for n in range(ts // SGU_LEN):
        v_stack = _stack_masked(v[n * SGU_LEN:(n + 1) * SGU_LEN, :]).astype(BF16)
        pieces.append(_dot(w_s, v_stack) + sgu_bias_ref[...])
    y_s = u * jnp.concatenate(pieces, axis=0)
    merged = merged_ref[...] + _branch_term(y_s, hb, 3, w_branch_ref, w_gate_ref, b_gate_ref)

    out = _dot(merged.astype(BF16), w_out_ref[...])
    o_ref[...] = _layer_norm_rows(ALPHA * h + out, ln_g_ref[...], ln_b_ref[...], LN_EPS)


def _retention_constants():
    cc = RET_CC
    heads = jnp.arange(RET_HEADS, dtype=F32)
    log_gamma = jnp.log1p(-(2.0 ** (-5.0 - heads)))
    idx = jnp.arange(cc, dtype=F32)
    diff = idx[:, None] - idx[None, :]
    decay = jnp.where(diff >= 0,
                      jnp.exp(log_gamma[:, None, None] * jnp.maximum(diff, 0.0)), 0.0)
    dcat = jnp.transpose(decay, (1, 0, 2)).reshape(cc, RET_HEADS * cc)
    zeta = jnp.exp(log_gamma[:, None] * (cc - 1 - idx)[None, :])
    xi = jnp.exp(log_gamma[:, None] * (idx + 1.0)[None, :])
    chunk_decay = jnp.exp(log_gamma * cc)
    zeta_l = jnp.repeat(zeta.T, RET_DK, axis=1)
    xi_l = jnp.repeat(xi.T, RET_DK, axis=1)
    head_of = jnp.arange(BRANCH_W) // RET_DK
    same = head_of[:, None] == head_of[None, :]
    bm = same.astype(F32)
    cd = jnp.where(same, chunk_decay[head_of][:, None], 0.0)
    avg = (bm / RET_DK).astype(BF16)
    return dcat, xi_l, zeta_l, cd, bm, avg


def _rope_tables(seq):
    half = RET_DK // 2
    inv = ROPE_BASE ** (-jnp.arange(half, dtype=F32) / half)
    ang = jnp.arange(seq).astype(F32)[:, None] * inv[None, :]
    cos = jnp.cos(ang)
    sin = jnp.sin(ang)
    cos_l = jnp.tile(jnp.concatenate([cos, cos], axis=1), (1, RET_HEADS))
    sin_l = jnp.tile(jnp.concatenate([-sin, sin], axis=1), (1, RET_HEADS))
    return cos_l, sin_l


def _mixer_layer(x3d, wts, p, consts, layer, cast_stacks, cast_layer):
    bsz, seq, _ = x3d.shape
    ts = MIX_TS
    nseq = seq // ts
    single = pl.Buffered(1)

    def lspec(shape):
        nd = len(shape)
        return pl.BlockSpec((None,) + shape, lambda b, s: (layer,) + (0,) * nd,
                            pipeline_mode=single)

    def cspec(shape):
        nd = len(shape)
        return pl.BlockSpec(shape, lambda b, s: (0,) * nd, pipeline_mode=single)

    casts = [_cast_io(st, cast_layer, bsz * nseq, lambda b, s: b * nseq + s) for st in cast_stacks]
    tile = pl.BlockSpec((None, ts, D_MODEL), lambda b, s: (b, s, 0))
    table = pl.BlockSpec((ts, BRANCH_W), lambda b, s: (s, 0))
    cos_l, sin_l, dcat, xi_l, zeta_l, cd, bm, avg = consts
    w_in, w_gate, w_branch, w_out = wts
    in_specs = [
        tile,
        cspec(w_in.shape), cspec(w_gate.shape), lspec((1, N_BRANCH * D_MODEL)),
        cspec(w_branch.shape), cspec(w_out.shape),
        lspec((3, BRANCH_W)), lspec((BRANCH_W, BRANCH_W)), lspec((1, BRANCH_W)),
        lspec((1, BRANCH_W)), lspec((1, BRANCH_W)), lspec((1, BRANCH_W)), lspec((1, BRANCH_W)),
        lspec((SGU_LEN, SGU_GROUPS * SGU_LEN)), lspec((SGU_LEN, BRANCH_W)),
        table, table,
        cspec(dcat.shape), cspec(xi_l.shape), cspec(zeta_l.shape),
        cspec(cd.shape), cspec(bm.shape), cspec(avg.shape),
        lspec((1, D_MODEL)), lspec((1, D_MODEL)),
    ] + [c[0] for c in casts]
    scratch = [
        pltpu.VMEM((ts, IN_COLS), F32),
        pltpu.VMEM((CONV_HIST + ts, BRANCH_W), F32),
        pltpu.VMEM((POOL_HIST + ts, BRANCH_W), F32),
        pltpu.VMEM((ts, BRANCH_W), F32),
        pltpu.VMEM((ts, BRANCH_W), F32),
        pltpu.VMEM((ts, BRANCH_W), F32),
        pltpu.VMEM((BRANCH_W, BRANCH_W), F32),
        pltpu.VMEM((ts, D_MODEL), F32),
    ]
    outs = pl.pallas_call(
        functools.partial(_mixer_kernel, len(casts)),
        grid=(bsz, nseq),
        in_specs=in_specs,
        out_specs=[tile] + [c[1] for c in casts],
        out_shape=[jax.ShapeDtypeStruct((bsz, seq, D_MODEL), F32)] + [c[2] for c in casts],
        scratch_shapes=scratch,
        compiler_params=pltpu.CompilerParams(
            dimension_semantics=("arbitrary", "arbitrary"), vmem_limit_bytes=VMEM_LIMIT),
        name="mixer_ln",
    )(x3d, w_in, w_gate, p["b_gate"], w_branch, w_out,
      p["conv_w"], p["pool_bd"], p["pool_scale"], p["gn_g"], p["gn_b"],
      p["sln_g"], p["sln_b"], p["sgu_w"], p["sgu_bias"],
      cos_l, sin_l, dcat, xi_l, zeta_l, cd, bm, avg, p["ln_g"], p["ln_b"], *cast_stacks)
    return outs[0], outs[1:]


def _prep_mixer_params(conv_w, pool_w, pool_scale, ret_gn_g, ret_gn_b, sgu_ln_g, sgu_ln_b,
                       sgu_w, sgu_b, b_gate, ln2_g, ln2_b):
    depth = conv_w.shape[0]
    groups = pool_w.shape[1]
    eye = jnp.eye(groups, dtype=pool_w.dtype)
    pool_bd = jnp.einsum("lgcd,gh->lgchd", pool_w, eye).reshape(depth, BRANCH_W, BRANCH_W)
    row = lambda a: a[:, None, :]
    return {
        "b_gate": row(b_gate),
        "conv_w": conv_w, "pool_bd": pool_bd.astype(BF16), "pool_scale": row(pool_scale),
        "gn_g": row(ret_gn_g), "gn_b": row(ret_gn_b),
        "sln_g": row(sgu_ln_g), "sln_b": row(sgu_ln_b),
        "sgu_w": jnp.transpose(sgu_w, (0, 2, 1, 3)).reshape(depth, SGU_LEN, SGU_GROUPS * SGU_LEN),
        "sgu_bias": jnp.repeat(jnp.transpose(sgu_b, (0, 2, 1)), BRANCH_W // SGU_GROUPS, axis=2),
        "ln_g": row(ln2_g), "ln_b": row(ln2_b),
    }


def kernel(x, ffn1_w1, ffn1_w2, ln1_g, ln1_b, w_in, conv_w, pool_w, pool_scale, ret_gn_g, ret_gn_b,
           sgu_ln_g, sgu_ln_b, sgu_w, sgu_b, w_branch, w_gate, b_gate, w_out, ln2_g, ln2_b,
           ffn2_w1, ffn2_w2, ln3_g, ln3_b):
    bsz, seq, d = x.shape
    depth = ffn1_w1.shape[0]
    row = lambda a: a[:, None, :]
    mp = _prep_mixer_params(conv_w, pool_w, pool_scale, ret_gn_g, ret_gn_b, sgu_ln_g, sgu_ln_b,
                            sgu_w, sgu_b, b_gate, ln2_g, ln2_b)
    consts = _rope_tables(seq) + _retention_constants()
    mixer_stacks = (w_in, w_gate, w_branch.reshape(depth, N_BRANCH * BRANCH_W, d), w_out)
    ffn1_stacks = (ffn1_w1, ffn1_w2)
    ffn2_stacks = (ffn2_w1, ffn2_w2)
    f1 = (ffn1_w1[0].astype(BF16), ffn1_w2[0].astype(BF16))
    x2 = x.reshape(bsz * seq, d)
    for l in range(depth):
        x2, mw = _ffn_layer(x2, f1[0], f1[1], row(ln1_g), row(ln1_b), l, mixer_stacks, l)
        x3, f2 = _mixer_layer(x2.reshape(bsz, seq, d), mw, mp, consts, l, ffn2_stacks, l)
        nxt = ffn1_stacks if l + 1 < depth else ()
        x2, f1 = _ffn_layer(x3.reshape(bsz * seq, d), f2[0], f2[1], row(ln3_g), row(ln3_b), l,
                            nxt, l + 1)
    return x2.reshape(bsz, seq, d)
```

```python
import functools
import math

import jax
import jax.numpy as jnp
import numpy as np
from jax import lax
from jax.experimental import pallas as pl
from jax.experimental.pallas import tpu as pltpu

F32 = jnp.float32
BF16 = jnp.bfloat16

D_MODEL = 1024
DEPTH = 4
CHUNK = 64
BRANCH_W = 256
N_BRANCH = 4
POOL_WINDOWS = (2, 4, 8, 16)
POOL_GC = BRANCH_W // len(POOL_WINDOWS)
RET_HEADS = 4
RET_DK = BRANCH_W // RET_HEADS
ROPE_BASE = 10000.0
SGU_LEN = 128
SGU_GROUPS = 4
D_FF = 2816
ALPHA = (2.0 * DEPTH) ** 0.25
LN_EPS = 1e-5
GN_EPS = 1e-5
IN_COLS = 10 * BRANCH_W

COL_BG, COL_CG, COL_XIN, COL_POOL = 0, 256, 512, 768
COL_Q, COL_K, COL_V, COL_G = 1024, 1280, 1536, 1792
COL_SU, COL_SV = 2048, 2304

FFN_SUBS = (256, 256, 256, 256)
FFN_TM = sum(FFN_SUBS)
MIX_SUBS = (256, 256)
MIX_TS = sum(MIX_SUBS)
RET_CC = 128
CONV_HIST = 8
POOL_HIST = 16
CAST_BLOCKS = 16
VMEM_LIMIT = 56 * 1024 * 1024


def _dot(a, b):
    return jnp.dot(a, b, preferred_element_type=F32)


def _layer_norm_rows(y, g, b, eps):
    mu = jnp.mean(y, axis=-1, keepdims=True)
    d = y - mu
    var = jnp.mean(d * d, axis=-1, keepdims=True)
    return d * lax.rsqrt(var + eps) * g + b


def _cast_io(stack, layer, steps, step_of):
    _, rows, cols = stack.shape
    blocks = math.gcd(steps, CAST_BLOCKS)
    slab = rows // blocks
    assert slab * blocks == rows and slab % 16 == 0
    per = steps // blocks
    in_spec = pl.BlockSpec((None, slab, cols), lambda *g: (layer, step_of(*g) // per, 0))
    out_spec = pl.BlockSpec((slab, cols), lambda *g: (step_of(*g) // per, 0))
    return in_spec, out_spec, jax.ShapeDtypeStruct((rows, cols), BF16)


def _convert_slabs(src_refs, dst_refs):
    for s_ref, d_ref in zip(src_refs, dst_refs, strict=True):
        d_ref[...] = s_ref[...].astype(BF16)


def _ffn_kernel(n_cast, layer, x_ref, w1_ref, w2_ref, g_ref, b_ref, *refs):
    cast_src, o_ref, cast_dst = refs[:n_cast], refs[n_cast], refs[n_cast + 1:]
    _convert_slabs(cast_src, cast_dst)
    ln_g = g_ref[layer:layer + 1, :]
    ln_b = b_ref[layer:layer + 1, :]
    lo = 0
    for rows_n in FFN_SUBS:
        rows = slice(lo, lo + rows_n)
        lo += rows_n
        x = x_ref[rows, :]
        xb = x.astype(BF16)
        gate = _dot(xb, w1_ref[:, 0:D_FF])
        up = _dot(xb, w1_ref[:, D_FF:2 * D_FF])
        act = (gate * jax.nn.sigmoid(gate) * up).astype(BF16)
        y = ALPHA * x + 0.5 * _dot(act, w2_ref[...])
        o_ref[rows, :] = _layer_norm_rows(y, ln_g, ln_b, LN_EPS)


def _ffn_layer(x2d, w1, w2, g_all, b_all, layer, cast_stacks, cast_layer):
    m = x2d.shape[0]
    steps = m // FFN_TM
    whole = lambda i: (0, 0)
    single = pl.Buffered(1)
    casts = [_cast_io(s, cast_layer, steps, lambda i: i) for s in cast_stacks]
    outs = pl.pallas_call(
        functools.partial(_ffn_kernel, len(casts), layer),
        grid=(steps,),
        in_specs=[
            pl.BlockSpec((FFN_TM, D_MODEL), lambda i: (i, 0)),
            pl.BlockSpec((D_MODEL, 2 * D_FF), whole, pipeline_mode=single),
            pl.BlockSpec((D_FF, D_MODEL), whole, pipeline_mode=single),
            pl.BlockSpec(g_all.shape, whole, pipeline_mode=single),
            pl.BlockSpec(b_all.shape, whole, pipeline_mode=single),
        ] + [c[0] for c in casts],
        out_specs=[pl.BlockSpec((FFN_TM, D_MODEL), lambda i: (i, 0))] + [c[1] for c in casts],
        out_shape=[jax.ShapeDtypeStruct((m, D_MODEL), F32)] + [c[2] for c in casts],
        compiler_params=pltpu.CompilerParams(
            dimension_semantics=("arbitrary",), vmem_limit_bytes=VMEM_LIMIT),
        name="ffn_ln",
    )(x2d, w1, w2, g_all, b_all, *cast_stacks)
    return outs[0], outs[1:]


def _lane_group_mask(shape, group):
    lane = lax.broadcasted_iota(jnp.int32, shape, 1)
    return (lane // RET_DK) == group


def _stack_masked(x):
    return jnp.concatenate(
        [jnp.where(_lane_group_mask(x.shape, g), x, 0.0) for g in range(RET_HEADS)], axis=0)


def _group_mean(x, avg_ref):
    hi = x.astype(BF16)
    lo = (x - hi.astype(F32)).astype(BF16)
    return _dot(hi, avg_ref[...]) + _dot(lo, avg_ref[...])


def _branch_gate(hb, n, w_gate_ref, b_gate):
    cols = slice(n * D_MODEL, (n + 1) * D_MODEL)
    return jax.nn.sigmoid(_dot(hb, w_gate_ref[:, cols]) + b_gate[:, cols])


def _branch_proj(y, n, w_branch_ref):
    return _dot(y.astype(BF16), w_branch_ref[n * BRANCH_W:(n + 1) * BRANCH_W, :])


def _mixer_kernel(n_cast, layer,
                  h_ref, w_in_ref, w_gate_ref, b_gate_ref, w_branch_ref, w_out_ref,
                  conv_w_ref, pool_bd_ref, pool_scale_ref, gn_g_ref, gn_b_ref,
                  sln_g_ref, sln_b_ref, sgu_w_ref, sgu_bias_ref,
                  cos_ref, sin_ref, icnt_ref, dcat_ref, xi_ref, zeta_ref, cd_ref, bm_ref, avg_ref,
                  ln_g_ref, ln_b_ref, *refs):
    cast_src, o_ref, cast_dst = refs[:n_cast], refs[n_cast], refs[n_cast + 1:2 * n_cast + 1]
    (z_ref, cext_ref, pext_ref, q_ref, k_ref, ret_ref, state_ref,
     merged_ref) = refs[2 * n_cast + 1:]
    ts = h_ref.shape[0]
    s = pl.program_id(1)

    _convert_slabs(cast_src, cast_dst)

    @pl.when(s == 0)
    def _():
        cext_ref[0:CONV_HIST, :] = jnp.zeros((CONV_HIST, BRANCH_W), F32)
        pext_ref[0:POOL_HIST, :] = jnp.zeros((POOL_HIST, BRANCH_W), F32)
        state_ref[...] = jnp.zeros_like(state_ref)

    wi = lax.broadcasted_iota(jnp.int32, (SGU_LEN, SGU_GROUPS * SGU_LEN), 0)
    wj = lax.broadcasted_iota(jnp.int32, (SGU_LEN, SGU_GROUPS * SGU_LEN), 1) % SGU_LEN
    w_s = jnp.where((wj // CHUNK) <= (wi // CHUNK), sgu_w_ref[...], 0.0).astype(BF16)
    cw = conv_w_ref[...]
    cd = cd_ref[...]
    bm = bm_ref[...]
    state = state_ref[...]
    layer_row = lambda ref: ref[layer:layer + 1, :]
    b_gate, pool_scale = layer_row(b_gate_ref), layer_row(pool_scale_ref)
    gn_g, gn_b = layer_row(gn_g_ref), layer_row(gn_b_ref)
    sln_g, sln_b = layer_row(sln_g_ref), layer_row(sln_b_ref)
    ln_g, ln_b = layer_row(ln_g_ref), layer_row(ln_b_ref)

    lo = 0
    for j, rn in enumerate(MIX_SUBS):
        rows = slice(lo, lo + rn)
        wr = slice((j % 2) * rn, (j % 2 + 1) * rn)
        h = h_ref[rows, :]
        hb = h.astype(BF16)
        z_ref[wr, :] = _dot(hb, w_in_ref[...])

        c = z_ref[wr, COL_CG:COL_CG + BRANCH_W] * z_ref[wr, COL_XIN:COL_XIN + BRANCH_W]
        at = CONV_HIST + lo
        cext_ref[at:at + rn, :] = c
        conv = (cw[0:1, :] * cext_ref[at - 2:at - 2 + rn, :]
                + cw[1:2, :] * cext_ref[at - 1:at - 1 + rn, :]
                + cw[2:3, :] * c)
        y_a = z_ref[wr, COL_BG:COL_BG + BRANCH_W] * conv
        merged_ref[wr, :] = (_branch_gate(hb, 0, w_gate_ref, b_gate)
                               * _branch_proj(y_a, 0, w_branch_ref))

        zp = z_ref[wr, COL_POOL:COL_POOL + BRANCH_W]
        pext_ref[POOL_HIST + lo:POOL_HIST + lo + rn, :] = zp
        half = BRANCH_W // 2
        low_group = lax.broadcasted_iota(jnp.int32, (rn, half), 1) < POOL_GC
        ext = pext_ref[lo:lo + POOL_HIST + rn, 0:half]
        s2 = ext + pltpu.roll(ext, 1, 0)
        s4 = s2 + pltpu.roll(s2, 2, 0)
        wsum_lo = jnp.where(low_group, s2[POOL_HIST:], s4[POOL_HIST:])
        ext = pext_ref[lo:lo + POOL_HIST + rn, half:BRANCH_W]
        s2 = ext + pltpu.roll(ext, 1, 0)
        s4 = s2 + pltpu.roll(s2, 2, 0)
        s8 = s4 + pltpu.roll(s4, 4, 0)
        s16 = s8 + pltpu.roll(s8, 8, 0)
        wsum_hi = jnp.where(low_group, s8[POOL_HIST:], s16[POOL_HIST:])
        wsum = jnp.concatenate([wsum_lo, wsum_hi], axis=1)
        lane = lax.broadcasted_iota(jnp.int32, (rn, BRANCH_W), 1)
        mixed = wsum * icnt_ref[rows, :] - zp
        y_p = _dot(mixed.astype(BF16), pool_bd_ref[...]) * pool_scale
        merged_ref[wr, :] += (_branch_gate(hb, 1, w_gate_ref, b_gate)
                                * _branch_proj(y_p, 1, w_branch_ref))

        cos = cos_ref[rows, :]
        sin = sin_ref[rows, :]
        first_half = (lane % RET_DK) < (RET_DK // 2)

        def rope(x):
            swapped = jnp.where(first_half, pltpu.roll(x, BRANCH_W - RET_DK // 2, 1),
                                pltpu.roll(x, RET_DK // 2, 1))
            return x * cos + swapped * sin

        q_ref[wr, :] = rope(z_ref[wr, COL_Q:COL_Q + BRANCH_W])
        k_ref[wr, :] = rope(z_ref[wr, COL_K:COL_K + BRANCH_W]) * (RET_DK ** -0.5)

        for n in range(rn // RET_CC):
            rr = slice(wr.start + n * RET_CC, wr.start + (n + 1) * RET_CC)
            qc = q_ref[rr, :]
            kc = k_ref[rr, :]
            vc = z_ref[rr, COL_V:COL_V + BRANCH_W]
            k_stack = _stack_masked(kc).astype(BF16)
            v_stack = _stack_masked(vc).astype(BF16)
            scores = lax.dot_general(qc.astype(BF16), k_stack, (((1,), (1,)), ((), ())),
                                     preferred_element_type=F32)
            probs = (scores * dcat_ref[...]).astype(BF16)
            inner = _dot(probs, v_stack)
            cross = _dot((qc * xi_ref[...]).astype(BF16), state.astype(BF16))
            ret_ref[rr, :] = inner + cross
            kz_t = jnp.transpose(kc * zeta_ref[...]).astype(BF16)
            kv = _dot(kz_t, vc.astype(BF16))
            state = state * cd + kv * bm

        o = ret_ref[wr, :]
        mu = _group_mean(o, avg_ref)
        d = o - mu
        var = _group_mean(d * d, avg_ref)
        o_n = d * lax.rsqrt(var + GN_EPS) * gn_g + gn_b
        gate_r = z_ref[wr, COL_G:COL_G + BRANCH_W]
        y_r = gate_r * jax.nn.sigmoid(gate_r) * o_n
        merged_ref[wr, :] += (_branch_gate(hb, 2, w_gate_ref, b_gate)
                                * _branch_proj(y_r, 2, w_branch_ref))

        u = jax.nn.gelu(z_ref[wr, COL_SU:COL_SU + BRANCH_W])
        v = jax.nn.gelu(z_ref[wr, COL_SV:COL_SV + BRANCH_W])
        v = _layer_norm_rows(v, sln_g, sln_b, LN_EPS)
        pieces = []
        for n in range(rn // SGU_LEN):
            v_stack = _stack_masked(v[n * SGU_LEN:(n + 1) * SGU_LEN, :]).astype(BF16)
            pieces.append(_dot(w_s, v_stack) + sgu_bias_ref[...])
        y_s = u * jnp.concatenate(pieces, axis=0)
        merged = merged_ref[wr, :] + (_branch_gate(hb, 3, w_gate_ref, b_gate)
                                        * _branch_proj(y_s, 3, w_branch_ref))

        out = _dot(merged.astype(BF16), w_out_ref[...])
        o_ref[rows, :] = _layer_norm_rows(ALPHA * h + out, ln_g, ln_b, LN_EPS)
        lo += rn

    cext_ref[0:CONV_HIST, :] = cext_ref[ts:ts + CONV_HIST, :]
    pext_ref[0:POOL_HIST, :] = pext_ref[ts:ts + POOL_HIST, :]
    state_ref[...] = state


def _retention_constants():
    cc = RET_CC
    heads = np.arange(RET_HEADS, dtype=np.float64)
    log_gamma = np.log1p(-(2.0 ** (-5.0 - heads)))
    idx = np.arange(cc, dtype=np.float64)
    diff = idx[:, None] - idx[None, :]
    decay = np.where(diff >= 0,
                     np.exp(log_gamma[:, None, None] * np.maximum(diff, 0.0)), 0.0)
    dcat = np.transpose(decay, (1, 0, 2)).reshape(cc, RET_HEADS * cc)
    zeta = np.exp(log_gamma[:, None] * (cc - 1 - idx)[None, :])
    xi = np.exp(log_gamma[:, None] * (idx + 1.0)[None, :])
    chunk_decay = np.exp(log_gamma * cc)
    zeta_l = np.repeat(zeta.T, RET_DK, axis=1)
    xi_l = np.repeat(xi.T, RET_DK, axis=1)
    head_of = np.arange(BRANCH_W) // RET_DK
    same = head_of[:, None] == head_of[None, :]
    bm = same.astype(np.float64)
    cd = np.where(same, chunk_decay[head_of][:, None], 0.0)
    f32 = lambda a: jnp.asarray(a, F32)
    return (f32(dcat), f32(xi_l), f32(zeta_l), f32(cd), f32(bm),
            jnp.asarray(bm / RET_DK, BF16))


def _rope_tables(seq):
    half = RET_DK // 2
    inv = ROPE_BASE ** (-jnp.arange(half, dtype=F32) / half)
    ang = jnp.arange(seq).astype(F32)[:, None] * inv[None, :]
    cos = jnp.cos(ang)
    sin = jnp.sin(ang)
    cos_l = jnp.tile(jnp.concatenate([cos, cos], axis=1), (1, RET_HEADS))
    sin_l = jnp.tile(jnp.concatenate([-sin, sin], axis=1), (1, RET_HEADS))
    return cos_l, sin_l


def _pool_inverse_counts(seq):
    win = jnp.repeat(jnp.asarray(POOL_WINDOWS, jnp.int32), POOL_GC)
    cnt = jnp.minimum(jnp.arange(seq, dtype=jnp.int32)[:, None] + 1, win[None, :])
    return 1.0 / cnt.astype(F32)


def _mixer_layer(x3d, wts, p, consts, layer, cast_stacks, cast_layer):
    bsz, seq, _ = x3d.shape
    ts = MIX_TS
    nseq = seq // ts
    single = pl.Buffered(1)

    def lspec(shape):
        nd = len(shape)
        return pl.BlockSpec((None,) + shape, lambda b, s: (layer,) + (0,) * nd,
                            pipeline_mode=single)

    def cspec(shape):
        nd = len(shape)
        return pl.BlockSpec(shape, lambda b, s: (0,) * nd, pipeline_mode=single)

    casts = [_cast_io(st, cast_layer, bsz * nseq, lambda b, s: b * nseq + s) for st in cast_stacks]
    tile = pl.BlockSpec((None, ts, D_MODEL), lambda b, s: (b, s, 0))
    table = pl.BlockSpec((ts, BRANCH_W), lambda b, s: (s, 0))
    cos_l, sin_l, icnt, dcat, xi_l, zeta_l, cd, bm, avg = consts
    w_in, w_gate, w_branch, w_out = wts
    in_specs = [
        tile,
        cspec(w_in.shape), cspec(w_gate.shape), cspec(p["b_gate"].shape),
        cspec(w_branch.shape), cspec(w_out.shape),
        lspec((3, BRANCH_W)), lspec((BRANCH_W, BRANCH_W)), cspec(p["pool_scale"].shape),
        cspec(p["gn_g"].shape), cspec(p["gn_b"].shape),
        cspec(p["sln_g"].shape), cspec(p["sln_b"].shape),
        lspec((SGU_LEN, SGU_GROUPS * SGU_LEN)), lspec((SGU_LEN, BRANCH_W)),
        table, table, table,
        cspec(dcat.shape), cspec(xi_l.shape), cspec(zeta_l.shape),
        cspec(cd.shape), cspec(bm.shape), cspec(avg.shape),
        cspec(p["ln_g"].shape), cspec(p["ln_b"].shape),
    ] + [c[0] for c in casts]
    assert len(set(MIX_SUBS)) == 1, "alternating work scratch needs equal sub-tiles"
    work = 2 * MIX_SUBS[0]
    scratch = [
        pltpu.VMEM((work, IN_COLS), F32),
        pltpu.VMEM((CONV_HIST + ts, BRANCH_W), F32),
        pltpu.VMEM((POOL_HIST + ts, BRANCH_W), F32),
        pltpu.VMEM((work, BRANCH_W), F32),
        pltpu.VMEM((work, BRANCH_W), F32),
        pltpu.VMEM((work, BRANCH_W), F32),
        pltpu.VMEM((BRANCH_W, BRANCH_W), F32),
        pltpu.VMEM((work, D_MODEL), F32),
    ]
    outs = pl.pallas_call(
        functools.partial(_mixer_kernel, len(casts), layer),
        grid=(bsz, nseq),
        in_specs=in_specs,
        out_specs=[tile] + [c[1] for c in casts],
        out_shape=[jax.ShapeDtypeStruct((bsz, seq, D_MODEL), F32)] + [c[2] for c in casts],
        scratch_shapes=scratch,
        compiler_params=pltpu.CompilerParams(
            dimension_semantics=("arbitrary", "arbitrary"), vmem_limit_bytes=VMEM_LIMIT),
        name="mixer_ln",
    )(x3d, w_in, w_gate, p["b_gate"], w_branch, w_out,
      p["conv_w"], p["pool_bd"], p["pool_scale"], p["gn_g"], p["gn_b"],
      p["sln_g"], p["sln_b"], p["sgu_w"], p["sgu_bias"],
      cos_l, sin_l, icnt, dcat, xi_l, zeta_l, cd, bm, avg, p["ln_g"], p["ln_b"], *cast_stacks)
    return outs[0], outs[1:]


def _prep_mixer_params(conv_w, pool_w, pool_scale, ret_gn_g, ret_gn_b, sgu_ln_g, sgu_ln_b,
                       sgu_w, sgu_b, b_gate, ln2_g, ln2_b):
    depth = conv_w.shape[0]
    groups = pool_w.shape[1]
    eye = jnp.eye(groups, dtype=pool_w.dtype)
    pool_bd = jnp.einsum("lgcd,gh->lgchd", pool_w, eye).reshape(depth, BRANCH_W, BRANCH_W)
    return {
        "b_gate": b_gate,
        "conv_w": conv_w, "pool_bd": pool_bd.astype(BF16), "pool_scale": pool_scale,
        "gn_g": ret_gn_g, "gn_b": ret_gn_b,
        "sln_g": sgu_ln_g, "sln_b": sgu_ln_b,
        "sgu_w": jnp.transpose(sgu_w, (0, 2, 1, 3)).reshape(depth, SGU_LEN, SGU_GROUPS * SGU_LEN),
        "sgu_bias": jnp.repeat(jnp.transpose(sgu_b, (0, 2, 1)), BRANCH_W // SGU_GROUPS, axis=2),
        "ln_g": ln2_g, "ln_b": ln2_b,
    }


def kernel(x, ffn1_w1, ffn1_w2, ln1_g, ln1_b, w_in, conv_w, pool_w, pool_scale, ret_gn_g, ret_gn_b,
           sgu_ln_g, sgu_ln_b, sgu_w, sgu_b, w_branch, w_gate, b_gate, w_out, ln2_g, ln2_b,
           ffn2_w1, ffn2_w2, ln3_g, ln3_b):
    bsz, seq, d = x.shape
    depth = ffn1_w1.shape[0]
    mp = _prep_mixer_params(conv_w, pool_w, pool_scale, ret_gn_g, ret_gn_b, sgu_ln_g, sgu_ln_b,
                            sgu_w, sgu_b, b_gate, ln2_g, ln2_b)
    consts = _rope_tables(seq) + (_pool_inverse_counts(seq),) + _retention_constants()
    mixer_stacks = (w_in, w_gate, w_branch.reshape(depth, N_BRANCH * BRANCH_W, d), w_out)
    ffn1_stacks = (ffn1_w1, ffn1_w2)
    ffn2_stacks = (ffn2_w1, ffn2_w2)
    f1 = (ffn1_w1[0].astype(BF16), ffn1_w2[0].astype(BF16))
    x2 = x.reshape(bsz * seq, d)
    for l in range(depth):
        x2, mw = _ffn_layer(x2, f1[0], f1[1], ln1_g, ln1_b, l, mixer_stacks, l)
        x3, f2 = _mixer_layer(x2.reshape(bsz, seq, d), mw, mp, consts, l, ffn2_stacks, l)
        nxt = ffn1_stacks if l + 1 < depth else ()
        x2, f1 = _ffn_layer(x3.reshape(bsz * seq, d), f2[0], f2[1], ln3_g, ln3_b, l, nxt, l + 1)
    return x2.reshape(bsz, seq, d)
```

```python
import functools
import math

import jax
import jax.numpy as jnp
import numpy as np
from jax import lax
from jax.experimental import pallas as pl
from jax.experimental.pallas import tpu as pltpu

F32 = jnp.float32
BF16 = jnp.bfloat16

D_MODEL = 1024
DEPTH = 4
CHUNK = 64
BRANCH_W = 256
N_BRANCH = 4
POOL_WINDOWS = (2, 4, 8, 16)
POOL_GC = BRANCH_W // len(POOL_WINDOWS)
RET_HEADS = 4
RET_DK = BRANCH_W // RET_HEADS
ROPE_BASE = 10000.0
SGU_LEN = 128
SGU_GROUPS = 4
D_FF = 2816
ALPHA = (2.0 * DEPTH) ** 0.25
LN_EPS = 1e-5
GN_EPS = 1e-5
IN_COLS = 10 * BRANCH_W

COL_BG, COL_CG, COL_XIN, COL_POOL = 0, 256, 512, 768
COL_Q, COL_K, COL_V, COL_G = 1024, 1280, 1536, 1792
COL_SU, COL_SV = 2048, 2304

FFN_SUBS = (256, 256, 256, 256)
FFN_TM = sum(FFN_SUBS)
MIX_SUBS = (256, 256)
MIX_TS = sum(MIX_SUBS)
RET_CC = 128
CONV_HIST = 8
POOL_HIST = 16
CAST_BLOCKS = 16
VMEM_LIMIT = 56 * 1024 * 1024


def _dot(a, b):
    return jnp.dot(a, b, preferred_element_type=F32)


def _layer_norm_rows(y, g, b, eps):
    mu = jnp.mean(y, axis=-1, keepdims=True)
    d = y - mu
    var = jnp.mean(d * d, axis=-1, keepdims=True)
    return d * lax.rsqrt(var + eps) * g + b


def _cast_io(stack, layer, steps, step_of):
    _, rows, cols = stack.shape
    blocks = math.gcd(steps, CAST_BLOCKS)
    slab = rows // blocks
    assert slab * blocks == rows and slab % 16 == 0
    per = steps // blocks
    in_spec = pl.BlockSpec((None, slab, cols), lambda *g: (layer, step_of(*g) // per, 0))
    out_spec = pl.BlockSpec((slab, cols), lambda *g: (step_of(*g) // per, 0))
    return in_spec, out_spec, jax.ShapeDtypeStruct((rows, cols), BF16)


def _prescale(w, scale):
    if isinstance(scale, tuple):
        split, left, right = scale
        col = lax.broadcasted_iota(jnp.int32, (1, w.shape[-1]), 1)
        return w * jnp.where(col < split, left, right).astype(w.dtype)
    return w if scale == 1.0 else w * scale


def _convert_slabs(src_refs, dst_refs, scales):
    for s_ref, d_ref, scale in zip(src_refs, dst_refs, scales, strict=True):
        d_ref[...] = _prescale(s_ref[...], scale).astype(BF16)


FFN_W1_SCALE = (D_FF, 0.5, 1.0)
FFN_W2_SCALE = 0.5
MIXER_SCALES = (1.0, 0.5, 0.5, 1.0)


def _ffn_kernel(cast_scales, layer, x_ref, w1_ref, w2_ref, g_ref, b_ref, *refs):
    n_cast = len(cast_scales)
    cast_src, o_ref, cast_dst = refs[:n_cast], refs[n_cast], refs[n_cast + 1:]
    _convert_slabs(cast_src, cast_dst, cast_scales)
    ln_g = g_ref[layer:layer + 1, :]
    ln_b = b_ref[layer:layer + 1, :]
    lo = 0
    for rows_n in FFN_SUBS:
        rows = slice(lo, lo + rows_n)
        lo += rows_n
        x = x_ref[rows, :]
        xb = x.astype(BF16)
        half_gate = _dot(xb, w1_ref[:, 0:D_FF])
        up = _dot(xb, w1_ref[:, D_FF:2 * D_FF])
        act = (half_gate * (1.0 + jnp.tanh(half_gate)) * up).astype(BF16)
        y = ALPHA * x + _dot(act, w2_ref[...])
        o_ref[rows, :] = _layer_norm_rows(y, ln_g, ln_b, LN_EPS)


def _ffn_layer(x2d, w1, w2, g_all, b_all, layer, cast_stacks, cast_scales, cast_layer):
    m = x2d.shape[0]
    steps = m // FFN_TM
    whole = lambda i: (0, 0)
    single = pl.Buffered(1)
    casts = [_cast_io(s, cast_layer, steps, lambda i: i) for s in cast_stacks]
    outs = pl.pallas_call(
        functools.partial(_ffn_kernel, tuple(cast_scales), layer),
        grid=(steps,),
        in_specs=[
            pl.BlockSpec((FFN_TM, D_MODEL), lambda i: (i, 0)),
            pl.BlockSpec((D_MODEL, 2 * D_FF), whole, pipeline_mode=single),
            pl.BlockSpec((D_FF, D_MODEL), whole, pipeline_mode=single),
            pl.BlockSpec(g_all.shape, whole, pipeline_mode=single),
            pl.BlockSpec(b_all.shape, whole, pipeline_mode=single),
        ] + [c[0] for c in casts],
        out_specs=[pl.BlockSpec((FFN_TM, D_MODEL), lambda i: (i, 0))] + [c[1] for c in casts],
        out_shape=[jax.ShapeDtypeStruct((m, D_MODEL), F32)] + [c[2] for c in casts],
        compiler_params=pltpu.CompilerParams(
            dimension_semantics=("arbitrary",), vmem_limit_bytes=VMEM_LIMIT),
        name="ffn_ln",
    )(x2d, w1, w2, g_all, b_all, *cast_stacks)
    return outs[0], outs[1:]


def _lane_group_mask(shape, group):
    lane = lax.broadcasted_iota(jnp.int32, shape, 1)
    return (lane // RET_DK) == group


def _stack_masked(x):
    return jnp.concatenate(
        [jnp.where(_lane_group_mask(x.shape, g), x, 0.0) for g in range(RET_HEADS)], axis=0)


def _head_columns(x_t):
    rows, cc = x_t.shape
    blocks = []
    for g in range(RET_HEADS):
        lo, hi = g * RET_DK, (g + 1) * RET_DK
        parts = [jnp.zeros((lo, cc), x_t.dtype)] if lo else []
        parts.append(x_t[lo:hi, :])
        if hi < rows:
            parts.append(jnp.zeros((rows - hi, cc), x_t.dtype))
        blocks.append(jnp.concatenate(parts, axis=0))
    return jnp.concatenate(blocks, axis=1)


def _group_mean(x, avg_ref):
    hi = x.astype(BF16)
    lo = (x - hi.astype(F32)).astype(BF16)
    return _dot(hi, avg_ref[...]) + _dot(lo, avg_ref[...])


def _branch_gate(hb, n, w_gate_ref, half_b_gate):
    cols = slice(n * D_MODEL, (n + 1) * D_MODEL)
    return 1.0 + jnp.tanh(_dot(hb, w_gate_ref[:, cols]) + half_b_gate[:, cols])


def _branch_proj(y, n, w_branch_ref):
    return _dot(y.astype(BF16), w_branch_ref[n * BRANCH_W:(n + 1) * BRANCH_W, :])


def _mixer_kernel(cast_scales, layer,
                  h_ref, w_in_ref, w_gate_ref, b_gate_ref, w_branch_ref, w_out_ref,
                  conv_w_ref, pool_bd_ref, pool_scale_ref, gn_g_ref, gn_b_ref,
                  sln_g_ref, sln_b_ref, sgu_w_ref, sgu_bias_ref,
                  cos_ref, sin_ref, icnt_ref, dcat_ref, xi_ref, zeta_ref, cd_ref, bm_ref, avg_ref,
                  ln_g_ref, ln_b_ref, *refs):
    n_cast = len(cast_scales)
    cast_src, o_ref, cast_dst = refs[:n_cast], refs[n_cast], refs[n_cast + 1:2 * n_cast + 1]
    (z_ref, cext_ref, pext_ref, q_ref, k_ref, ret_ref, state_ref,
     merged_ref) = refs[2 * n_cast + 1:]
    ts = h_ref.shape[0]
    s = pl.program_id(1)

    _convert_slabs(cast_src, cast_dst, cast_scales)

    @pl.when(s == 0)
    def _():
        cext_ref[0:CONV_HIST, :] = jnp.zeros((CONV_HIST, BRANCH_W), F32)
        pext_ref[0:POOL_HIST, :] = jnp.zeros((POOL_HIST, BRANCH_W), F32)
        state_ref[...] = jnp.zeros_like(state_ref)

    wi = lax.broadcasted_iota(jnp.int32, (SGU_LEN, SGU_GROUPS * SGU_LEN), 0)
    wj = lax.broadcasted_iota(jnp.int32, (SGU_LEN, SGU_GROUPS * SGU_LEN), 1) % SGU_LEN
    w_s = jnp.where((wj // CHUNK) <= (wi // CHUNK), sgu_w_ref[...], 0.0).astype(BF16)
    cw = conv_w_ref[...]
    cd = cd_ref[...]
    bm = bm_ref[...]
    state = state_ref[...]
    layer_row = lambda ref: ref[layer:layer + 1, :]
    b_gate, pool_scale = 0.5 * layer_row(b_gate_ref), layer_row(pool_scale_ref)
    gn_g, gn_b = layer_row(gn_g_ref), layer_row(gn_b_ref)
    sln_g, sln_b = layer_row(sln_g_ref), layer_row(sln_b_ref)
    ln_g, ln_b = layer_row(ln_g_ref), layer_row(ln_b_ref)

    lo = 0
    for j, rn in enumerate(MIX_SUBS):
        rows = slice(lo, lo + rn)
        wr = slice((j % 2) * rn, (j % 2 + 1) * rn)
        h = h_ref[rows, :]
        hb = h.astype(BF16)
        z_ref[wr, :] = _dot(hb, w_in_ref[...])

        c = z_ref[wr, COL_CG:COL_CG + BRANCH_W] * z_ref[wr, COL_XIN:COL_XIN + BRANCH_W]
        at = CONV_HIST + lo
        cext_ref[at:at + rn, :] = c
        conv = (cw[0:1, :] * cext_ref[at - 2:at - 2 + rn, :]
                + cw[1:2, :] * cext_ref[at - 1:at - 1 + rn, :]
                + cw[2:3, :] * c)
        y_a = z_ref[wr, COL_BG:COL_BG + BRANCH_W] * conv
        merged_ref[wr, :] = (_branch_gate(hb, 0, w_gate_ref, b_gate)
                               * _branch_proj(y_a, 0, w_branch_ref))

        zp = z_ref[wr, COL_POOL:COL_POOL + BRANCH_W]
        pext_ref[POOL_HIST + lo:POOL_HIST + lo + rn, :] = zp
        half = BRANCH_W // 2
        low_group = lax.broadcasted_iota(jnp.int32, (rn, half), 1) < POOL_GC
        ext = pext_ref[lo:lo + POOL_HIST + rn, 0:half]
        s2 = ext + pltpu.roll(ext, 1, 0)
        s4 = s2 + pltpu.roll(s2, 2, 0)
        wsum_lo = jnp.where(low_group, s2[POOL_HIST:], s4[POOL_HIST:])
        ext = pext_ref[lo:lo + POOL_HIST + rn, half:BRANCH_W]
        s2 = ext + pltpu.roll(ext, 1, 0)
        s4 = s2 + pltpu.roll(s2, 2, 0)
        s8 = s4 + pltpu.roll(s4, 4, 0)
        s16 = s8 + pltpu.roll(s8, 8, 0)
        wsum_hi = jnp.where(low_group, s8[POOL_HIST:], s16[POOL_HIST:])
        wsum = jnp.concatenate([wsum_lo, wsum_hi], axis=1)
        lane = lax.broadcasted_iota(jnp.int32, (rn, BRANCH_W), 1)
        mixed = wsum * icnt_ref[rows, :] - zp
        y_p = _dot(mixed.astype(BF16), pool_bd_ref[...]) * pool_scale
        merged_ref[wr, :] += (_branch_gate(hb, 1, w_gate_ref, b_gate)
                                * _branch_proj(y_p, 1, w_branch_ref))

        cos = cos_ref[rows, :]
        sin = sin_ref[rows, :]
        first_half = (lane % RET_DK) < (RET_DK // 2)

        def rope(x):
            swapped = jnp.where(first_half, pltpu.roll(x, BRANCH_W - RET_DK // 2, 1),
                                pltpu.roll(x, RET_DK // 2, 1))
            return x * cos + swapped * sin

        q_ref[wr, :] = rope(z_ref[wr, COL_Q:COL_Q + BRANCH_W])
        k_ref[wr, :] = rope(z_ref[wr, COL_K:COL_K + BRANCH_W]) * (RET_DK ** -0.5)

        for n in range(rn // RET_CC):
            rr = slice(wr.start + n * RET_CC, wr.start + (n + 1) * RET_CC)
            qc = q_ref[rr, :]
            kc = k_ref[rr, :]
            vc = z_ref[rr, COL_V:COL_V + BRANCH_W]
            k_t = jnp.transpose(kc)
            v_stack = _stack_masked(vc).astype(BF16)
            scores = _dot(qc.astype(BF16), _head_columns(k_t).astype(BF16))
            probs = (scores * dcat_ref[...]).astype(BF16)
            inner = _dot(probs, v_stack)
            cross = _dot((qc * xi_ref[...]).astype(BF16), state.astype(BF16))
            ret_ref[rr, :] = inner + cross
            kv = _dot((k_t * zeta_ref[...]).astype(BF16), vc.astype(BF16))
            state = state * cd + kv * bm

        o = ret_ref[wr, :]
        mu = _group_mean(o, avg_ref)
        d = o - mu
        var = _group_mean(d * d, avg_ref)
        o_n = d * lax.rsqrt(var + GN_EPS) * gn_g + gn_b
        gate_r = z_ref[wr, COL_G:COL_G + BRANCH_W]
        y_r = gate_r * jax.nn.sigmoid(gate_r) * o_n
        merged_ref[wr, :] += (_branch_gate(hb, 2, w_gate_ref, b_gate)
                                * _branch_proj(y_r, 2, w_branch_ref))

        u = jax.nn.gelu(z_ref[wr, COL_SU:COL_SU + BRANCH_W])
        v = jax.nn.gelu(z_ref[wr, COL_SV:COL_SV + BRANCH_W])
        v = _layer_norm_rows(v, sln_g, sln_b, LN_EPS)
        pieces = []
        for n in range(rn // SGU_LEN):
            v_stack = _stack_masked(v[n * SGU_LEN:(n + 1) * SGU_LEN, :]).astype(BF16)
            pieces.append(_dot(w_s, v_stack) + sgu_bias_ref[...])
        y_s = u * jnp.concatenate(pieces, axis=0)
        merged = merged_ref[wr, :] + (_branch_gate(hb, 3, w_gate_ref, b_gate)
                                        * _branch_proj(y_s, 3, w_branch_ref))

        out = _dot(merged.astype(BF16), w_out_ref[...])
        o_ref[rows, :] = _layer_norm_rows(ALPHA * h + out, ln_g, ln_b, LN_EPS)
        lo += rn

    cext_ref[0:CONV_HIST, :] = cext_ref[ts:ts + CONV_HIST, :]
    pext_ref[0:POOL_HIST, :] = pext_ref[ts:ts + POOL_HIST, :]
    state_ref[...] = state


def _retention_constants():
    cc = RET_CC
    heads = np.arange(RET_HEADS, dtype=np.float64)
    log_gamma = np.log1p(-(2.0 ** (-5.0 - heads)))
    idx = np.arange(cc, dtype=np.float64)
    diff = idx[:, None] - idx[None, :]
    decay = np.where(diff >= 0,
                     np.exp(log_gamma[:, None, None] * np.maximum(diff, 0.0)), 0.0)
    dcat = np.transpose(decay, (1, 0, 2)).reshape(cc, RET_HEADS * cc)
    zeta = np.exp(log_gamma[:, None] * (cc - 1 - idx)[None, :])
    xi = np.exp(log_gamma[:, None] * (idx + 1.0)[None, :])
    chunk_decay = np.exp(log_gamma * cc)
    zeta_l = np.repeat(zeta.T, RET_DK, axis=1)
    xi_l = np.repeat(xi.T, RET_DK, axis=1)
    head_of = np.arange(BRANCH_W) // RET_DK
    same = head_of[:, None] == head_of[None, :]
    bm = same.astype(np.float64)
    cd = np.where(same, chunk_decay[head_of][:, None], 0.0)
    f32 = lambda a: jnp.asarray(a, F32)
    return (f32(dcat), f32(xi_l), f32(zeta_l.T), f32(cd), f32(bm),
            jnp.asarray(bm / RET_DK, BF16))


def _rope_tables(seq):
    half = RET_DK // 2
    inv = ROPE_BASE ** (-jnp.arange(half, dtype=F32) / half)
    ang = jnp.arange(seq).astype(F32)[:, None] * inv[None, :]
    cos = jnp.cos(ang)
    sin = jnp.sin(ang)
    cos_l = jnp.tile(jnp.concatenate([cos, cos], axis=1), (1, RET_HEADS))
    sin_l = jnp.tile(jnp.concatenate([-sin, sin], axis=1), (1, RET_HEADS))
    return cos_l, sin_l


def _pool_inverse_counts(seq):
    win = jnp.repeat(jnp.asarray(POOL_WINDOWS, jnp.int32), POOL_GC)
    cnt = jnp.minimum(jnp.arange(seq, dtype=jnp.int32)[:, None] + 1, win[None, :])
    return 1.0 / cnt.astype(F32)


def _mixer_layer(x3d, wts, p, consts, layer, cast_stacks, cast_scales, cast_layer):
    bsz, seq, _ = x3d.shape
    ts = MIX_TS
    nseq = seq // ts
    single = pl.Buffered(1)

    def lspec(shape):
        nd = len(shape)
        return pl.BlockSpec((None,) + shape, lambda b, s: (layer,) + (0,) * nd,
                            pipeline_mode=single)

    def cspec(shape):
        nd = len(shape)
        return pl.BlockSpec(shape, lambda b, s: (0,) * nd, pipeline_mode=single)

    casts = [_cast_io(st, cast_layer, bsz * nseq, lambda b, s: b * nseq + s) for st in cast_stacks]
    tile = pl.BlockSpec((None, ts, D_MODEL), lambda b, s: (b, s, 0))
    table = pl.BlockSpec((ts, BRANCH_W), lambda b, s: (s, 0))
    cos_l, sin_l, icnt, dcat, xi_l, zeta_l, cd, bm, avg = consts
    w_in, w_gate, w_branch, w_out = wts
    in_specs = [
        tile,
        cspec(w_in.shape), cspec(w_gate.shape), cspec(p["b_gate"].shape),
        cspec(w_branch.shape), cspec(w_out.shape),
        lspec((3, BRANCH_W)), lspec((BRANCH_W, BRANCH_W)), cspec(p["pool_scale"].shape),
        cspec(p["gn_g"].shape), cspec(p["gn_b"].shape),
        cspec(p["sln_g"].shape), cspec(p["sln_b"].shape),
        lspec((SGU_LEN, SGU_GROUPS * SGU_LEN)), lspec((SGU_LEN, BRANCH_W)),
        table, table, table,
        cspec(dcat.shape), cspec(xi_l.shape), cspec(zeta_l.shape),
        cspec(cd.shape), cspec(bm.shape), cspec(avg.shape),
        cspec(p["ln_g"].shape), cspec(p["ln_b"].shape),
    ] + [c[0] for c in casts]
    assert len(set(MIX_SUBS)) == 1, "alternating work scratch needs equal sub-tiles"
    work = 2 * MIX_SUBS[0]
    scratch = [
        pltpu.VMEM((work, IN_COLS), F32),
        pltpu.VMEM((CONV_HIST + ts, BRANCH_W), F32),
        pltpu.VMEM((POOL_HIST + ts, BRANCH_W), F32),
        pltpu.VMEM((work, BRANCH_W), F32),
        pltpu.VMEM((work, BRANCH_W), F32),
        pltpu.VMEM((work, BRANCH_W), F32),
        pltpu.VMEM((BRANCH_W, BRANCH_W), F32),
        pltpu.VMEM((work, D_MODEL), F32),
    ]
    outs = pl.pallas_call(
        functools.partial(_mixer_kernel, tuple(cast_scales), layer),
        grid=(bsz, nseq),
        in_specs=in_specs,
        out_specs=[tile] + [c[1] for c in casts],
        out_shape=[jax.ShapeDtypeStruct((bsz, seq, D_MODEL), F32)] + [c[2] for c in casts],
        scratch_shapes=scratch,
        compiler_params=pltpu.CompilerParams(
            dimension_semantics=("arbitrary", "arbitrary"), vmem_limit_bytes=VMEM_LIMIT),
        name="mixer_ln",
    )(x3d, w_in, w_gate, p["b_gate"], w_branch, w_out,
      p["conv_w"], p["pool_bd"], p["pool_scale"], p["gn_g"], p["gn_b"],
      p["sln_g"], p["sln_b"], p["sgu_w"], p["sgu_bias"],
      cos_l, sin_l, icnt, dcat, xi_l, zeta_l, cd, bm, avg, p["ln_g"], p["ln_b"], *cast_stacks)
    return outs[0], outs[1:]


def _prep_mixer_params(conv_w, pool_w, pool_scale, ret_gn_g, ret_gn_b, sgu_ln_g, sgu_ln_b,
                       sgu_w, sgu_b, b_gate, ln2_g, ln2_b):
    depth = conv_w.shape[0]
    groups = pool_w.shape[1]
    eye = jnp.eye(groups, dtype=pool_w.dtype)
    pool_bd = jnp.einsum("lgcd,gh->lgchd", pool_w, eye).reshape(depth, BRANCH_W, BRANCH_W)
    return {
        "b_gate": b_gate,
        "conv_w": conv_w, "pool_bd": pool_bd.astype(BF16), "pool_scale": pool_scale,
        "gn_g": ret_gn_g, "gn_b": ret_gn_b,
        "sln_g": sgu_ln_g, "sln_b": sgu_ln_b,
        "sgu_w": jnp.transpose(sgu_w, (0, 2, 1, 3)).reshape(depth, SGU_LEN, SGU_GROUPS * SGU_LEN),
        "sgu_bias": jnp.repeat(jnp.transpose(sgu_b, (0, 2, 1)), BRANCH_W // SGU_GROUPS, axis=2),
        "ln_g": ln2_g, "ln_b": ln2_b,
    }


def kernel(x, ffn1_w1, ffn1_w2, ln1_g, ln1_b, w_in, conv_w, pool_w, pool_scale, ret_gn_g, ret_gn_b,
           sgu_ln_g, sgu_ln_b, sgu_w, sgu_b, w_branch, w_gate, b_gate, w_out, ln2_g, ln2_b,
           ffn2_w1, ffn2_w2, ln3_g, ln3_b):
    bsz, seq, d = x.shape
    depth = ffn1_w1.shape[0]
    mp = _prep_mixer_params(conv_w, pool_w, pool_scale, ret_gn_g, ret_gn_b, sgu_ln_g, sgu_ln_b,
                            sgu_w, sgu_b, b_gate, ln2_g, ln2_b)
    consts = _rope_tables(seq) + (_pool_inverse_counts(seq),) + _retention_constants()
    mixer_stacks = (w_in, w_gate, w_branch.reshape(depth, N_BRANCH * BRANCH_W, d), w_out)
    ffn1_stacks = (ffn1_w1, ffn1_w2)
    ffn2_stacks = (ffn2_w1, ffn2_w2)
    ffn_scales = (FFN_W1_SCALE, FFN_W2_SCALE)
    f1 = (_prescale(ffn1_w1[0], FFN_W1_SCALE).astype(BF16),
          _prescale(ffn1_w2[0], FFN_W2_SCALE).astype(BF16))
    x2 = x.reshape(bsz * seq, d)
    for l in range(depth):
        x2, mw = _ffn_layer(x2, f1[0], f1[1], ln1_g, ln1_b, l, mixer_stacks, MIXER_SCALES, l)
        x3, f2 = _mixer_layer(x2.reshape(bsz, seq, d), mw, mp, consts, l,
                              ffn2_stacks, ffn_scales, l)
        last = l + 1 == depth
        x2, f1 = _ffn_layer(x3.reshape(bsz * seq, d), f2[0], f2[1], ln3_g, ln3_b, l,
                            () if last else ffn1_stacks, () if last else ffn_scales, l + 1)
    return x2.reshape(bsz, seq, d)
```

```python
import functools
import math

import jax
import jax.numpy as jnp
import numpy as np
from jax import lax
from jax.experimental import pallas as pl
from jax.experimental.pallas import tpu as pltpu

F32 = jnp.float32
BF16 = jnp.bfloat16

D_MODEL = 1024
DEPTH = 4
CHUNK = 64
BRANCH_W = 256
N_BRANCH = 4
POOL_WINDOWS = (2, 4, 8, 16)
POOL_GC = BRANCH_W // len(POOL_WINDOWS)
RET_HEADS = 4
RET_DK = BRANCH_W // RET_HEADS
ROPE_BASE = 10000.0
SGU_LEN = 128
SGU_GROUPS = 4
D_FF = 2816
ALPHA = (2.0 * DEPTH) ** 0.25
LN_EPS = 1e-5
GN_EPS = 1e-5
IN_COLS = 10 * BRANCH_W

COL_BG, COL_CG, COL_XIN, COL_POOL = 0, 256, 512, 768
COL_Q, COL_K, COL_V, COL_G = 1024, 1280, 1536, 1792
COL_SU, COL_SV = 2048, 2304

FFN_SUBS = (256, 256, 256, 256)
FFN_TM = sum(FFN_SUBS)
MIX_SUBS = (256, 256, 256, 256)
MIX_TS = sum(MIX_SUBS)
RET_CC = 128
CONV_HIST = 8
POOL_HIST = 16
CAST_BLOCKS = 16
VMEM_LIMIT = 56 * 1024 * 1024


def _dot(a, b):
    return jnp.dot(a, b, preferred_element_type=F32)


def _layer_norm_rows(y, g, b, eps):
    mu = jnp.mean(y, axis=-1, keepdims=True)
    d = y - mu
    var = jnp.mean(d * d, axis=-1, keepdims=True)
    return d * lax.rsqrt(var + eps) * g + b


def _cast_io(stack, layer, steps, step_of):
    _, rows, cols = stack.shape
    blocks = math.gcd(steps, CAST_BLOCKS)
    slab = rows // blocks
    assert slab * blocks == rows and slab % 16 == 0
    per = steps // blocks
    in_spec = pl.BlockSpec((None, slab, cols), lambda *g: (layer, step_of(*g) // per, 0))
    out_spec = pl.BlockSpec((slab, cols), lambda *g: (step_of(*g) // per, 0))
    return in_spec, out_spec, jax.ShapeDtypeStruct((rows, cols), BF16)


def _prescale(w, scale):
    if isinstance(scale, tuple):
        split, left, right = scale
        col = lax.broadcasted_iota(jnp.int32, (1, w.shape[-1]), 1)
        return w * jnp.where(col < split, left, right).astype(w.dtype)
    return w if scale == 1.0 else w * scale


def _convert_slabs(src_refs, dst_refs, scales):
    for s_ref, d_ref, scale in zip(src_refs, dst_refs, scales, strict=True):
        d_ref[...] = _prescale(s_ref[...], scale).astype(BF16)


FFN_W1_SCALE = (D_FF, 0.5, 1.0)
FFN_W2_SCALE = 0.5
MIXER_SCALES = (1.0, 0.5, 0.5, 1.0)


def _ffn_kernel(cast_scales, layer, x_ref, w1_ref, w2_ref, g_ref, b_ref, *refs):
    n_cast = len(cast_scales)
    cast_src, o_ref, cast_dst = refs[:n_cast], refs[n_cast], refs[n_cast + 1:]
    _convert_slabs(cast_src, cast_dst, cast_scales)
    ln_g = g_ref[layer:layer + 1, :]
    ln_b = b_ref[layer:layer + 1, :]
    lo = 0
    for rows_n in FFN_SUBS:
        rows = slice(lo, lo + rows_n)
        lo += rows_n
        x = x_ref[rows, :]
        xb = x.astype(BF16)
        half_gate = _dot(xb, w1_ref[:, 0:D_FF])
        up = _dot(xb, w1_ref[:, D_FF:2 * D_FF])
        act = (half_gate * (1.0 + jnp.tanh(half_gate)) * up).astype(BF16)
        y = ALPHA * x + _dot(act, w2_ref[...])
        o_ref[rows, :] = _layer_norm_rows(y, ln_g, ln_b, LN_EPS)


def _ffn_layer(x2d, w1, w2, g_all, b_all, layer, cast_stacks, cast_scales, cast_layer):
    m = x2d.shape[0]
    steps = m // FFN_TM
    whole = lambda i: (0, 0)
    single = pl.Buffered(1)
    casts = [_cast_io(s, cast_layer, steps, lambda i: i) for s in cast_stacks]
    outs = pl.pallas_call(
        functools.partial(_ffn_kernel, tuple(cast_scales), layer),
        grid=(steps,),
        in_specs=[
            pl.BlockSpec((FFN_TM, D_MODEL), lambda i: (i, 0)),
            pl.BlockSpec((D_MODEL, 2 * D_FF), whole, pipeline_mode=single),
            pl.BlockSpec((D_FF, D_MODEL), whole, pipeline_mode=single),
            pl.BlockSpec(g_all.shape, whole, pipeline_mode=single),
            pl.BlockSpec(b_all.shape, whole, pipeline_mode=single),
        ] + [c[0] for c in casts],
        out_specs=[pl.BlockSpec((FFN_TM, D_MODEL), lambda i: (i, 0))] + [c[1] for c in casts],
        out_shape=[jax.ShapeDtypeStruct((m, D_MODEL), F32)] + [c[2] for c in casts],
        compiler_params=pltpu.CompilerParams(
            dimension_semantics=("arbitrary",), vmem_limit_bytes=VMEM_LIMIT),
        name="ffn_ln",
    )(x2d, w1, w2, g_all, b_all, *cast_stacks)
    return outs[0], outs[1:]


def _lane_group_mask(shape, group):
    lane = lax.broadcasted_iota(jnp.int32, shape, 1)
    return (lane // RET_DK) == group


def _stack_masked(x):
    return jnp.concatenate(
        [jnp.where(_lane_group_mask(x.shape, g), x, 0.0) for g in range(RET_HEADS)], axis=0)


def _head_columns(x_t):
    rows, cc = x_t.shape
    blocks = []
    for g in range(RET_HEADS):
        lo, hi = g * RET_DK, (g + 1) * RET_DK
        parts = [jnp.zeros((lo, cc), x_t.dtype)] if lo else []
        parts.append(x_t[lo:hi, :])
        if hi < rows:
            parts.append(jnp.zeros((rows - hi, cc), x_t.dtype))
        blocks.append(jnp.concatenate(parts, axis=0))
    return jnp.concatenate(blocks, axis=1)


def _group_mean(x, avg_ref):
    hi = x.astype(BF16)
    lo = (x - hi.astype(F32)).astype(BF16)
    return _dot(hi, avg_ref[...]) + _dot(lo, avg_ref[...])


def _branch_gate(hb, n, w_gate_ref, half_b_gate):
    cols = slice(n * D_MODEL, (n + 1) * D_MODEL)
    return 1.0 + jnp.tanh(_dot(hb, w_gate_ref[:, cols]) + half_b_gate[:, cols])


def _branch_proj(y, n, w_branch_ref):
    return _dot(y.astype(BF16), w_branch_ref[n * BRANCH_W:(n + 1) * BRANCH_W, :])


def _mixer_kernel(cast_scales, layer,
                  h_ref, w_in_ref, w_gate_ref, b_gate_ref, w_branch_ref, w_out_ref,
                  conv_w_ref, pool_bd_ref, pool_scale_ref, gn_g_ref, gn_b_ref,
                  sln_g_ref, sln_b_ref, sgu_w_ref, sgu_bias_ref,
                  cos_ref, sin_ref, icnt_ref, dcat_ref, xi_ref, zeta_ref, cd_ref, bm_ref, avg_ref,
                  ln_g_ref, ln_b_ref, *refs):
    n_cast = len(cast_scales)
    cast_src, o_ref, cast_dst = refs[:n_cast], refs[n_cast], refs[n_cast + 1:2 * n_cast + 1]
    (z_ref, cext_ref, pext_ref, q_ref, k_ref, ret_ref, state_ref,
     merged_ref) = refs[2 * n_cast + 1:]
    ts = h_ref.shape[0]
    s = pl.program_id(1)

    _convert_slabs(cast_src, cast_dst, cast_scales)

    @pl.when(s == 0)
    def _():
        cext_ref[0:CONV_HIST, :] = jnp.zeros((CONV_HIST, BRANCH_W), F32)
        pext_ref[0:POOL_HIST, :] = jnp.zeros((POOL_HIST, BRANCH_W), F32)
        state_ref[...] = jnp.zeros_like(state_ref)

    wi = lax.broadcasted_iota(jnp.int32, (SGU_LEN, SGU_GROUPS * SGU_LEN), 0)
    wj = lax.broadcasted_iota(jnp.int32, (SGU_LEN, SGU_GROUPS * SGU_LEN), 1) % SGU_LEN
    w_s = jnp.where((wj // CHUNK) <= (wi // CHUNK), sgu_w_ref[...], 0.0).astype(BF16)
    cw = conv_w_ref[...]
    cd = cd_ref[...]
    bm = bm_ref[...]
    state = state_ref[...]
    layer_row = lambda ref: ref[layer:layer + 1, :]
    b_gate, pool_scale = 0.5 * layer_row(b_gate_ref), layer_row(pool_scale_ref)
    gn_g, gn_b = layer_row(gn_g_ref), layer_row(gn_b_ref)
    sln_g, sln_b = layer_row(sln_g_ref), layer_row(sln_b_ref)
    ln_g, ln_b = layer_row(ln_g_ref), layer_row(ln_b_ref)

    lo = 0
    for j, rn in enumerate(MIX_SUBS):
        rows = slice(lo, lo + rn)
        wr = slice((j % 2) * rn, (j % 2 + 1) * rn)
        h = h_ref[rows, :]
        hb = h.astype(BF16)
        z_ref[wr, :] = _dot(hb, w_in_ref[...])

        c = z_ref[wr, COL_CG:COL_CG + BRANCH_W] * z_ref[wr, COL_XIN:COL_XIN + BRANCH_W]
        at = CONV_HIST + lo
        cext_ref[at:at + rn, :] = c
        conv = (cw[0:1, :] * cext_ref[at - 2:at - 2 + rn, :]
                + cw[1:2, :] * cext_ref[at - 1:at - 1 + rn, :]
                + cw[2:3, :] * c)
        y_a = z_ref[wr, COL_BG:COL_BG + BRANCH_W] * conv
        merged_ref[wr, :] = (_branch_gate(hb, 0, w_gate_ref, b_gate)
                               * _branch_proj(y_a, 0, w_branch_ref))

        zp = z_ref[wr, COL_POOL:COL_POOL + BRANCH_W]
        pext_ref[POOL_HIST + lo:POOL_HIST + lo + rn, :] = zp
        half = BRANCH_W // 2
        low_group = lax.broadcasted_iota(jnp.int32, (rn, half), 1) < POOL_GC
        ext = pext_ref[lo:lo + POOL_HIST + rn, 0:half]
        s2 = ext + pltpu.roll(ext, 1, 0)
        s4 = s2 + pltpu.roll(s2, 2, 0)
        wsum_lo = jnp.where(low_group, s2[POOL_HIST:], s4[POOL_HIST:])
        ext = pext_ref[lo:lo + POOL_HIST + rn, half:BRANCH_W]
        s2 = ext + pltpu.roll(ext, 1, 0)
        s4 = s2 + pltpu.roll(s2, 2, 0)
        s8 = s4 + pltpu.roll(s4, 4, 0)
        s16 = s8 + pltpu.roll(s8, 8, 0)
        wsum_hi = jnp.where(low_group, s8[POOL_HIST:], s16[POOL_HIST:])
        wsum = jnp.concatenate([wsum_lo, wsum_hi], axis=1)
        lane = lax.broadcasted_iota(jnp.int32, (rn, BRANCH_W), 1)
        mixed = wsum * icnt_ref[POOL_HIST:POOL_HIST + 1, :] - zp
        if j == 0:
            inv_cnt = jnp.where(s == 0, icnt_ref[0:POOL_HIST, :], icnt_ref[POOL_HIST:, :])
            mixed = jnp.concatenate(
                [wsum[0:POOL_HIST, :] * inv_cnt - zp[0:POOL_HIST, :], mixed[POOL_HIST:, :]], axis=0)
        y_p = _dot(mixed.astype(BF16), pool_bd_ref[...]) * pool_scale
        merged_ref[wr, :] += (_branch_gate(hb, 1, w_gate_ref, b_gate)
                                * _branch_proj(y_p, 1, w_branch_ref))

        cos = cos_ref[rows, :]
        sin = sin_ref[rows, :]
        first_half = (lane % RET_DK) < (RET_DK // 2)

        def rope(x):
            swapped = jnp.where(first_half, pltpu.roll(x, BRANCH_W - RET_DK // 2, 1),
                                pltpu.roll(x, RET_DK // 2, 1))
            return x * cos + swapped * sin

        q_ref[wr, :] = rope(z_ref[wr, COL_Q:COL_Q + BRANCH_W])
        k_ref[wr, :] = rope(z_ref[wr, COL_K:COL_K + BRANCH_W]) * (RET_DK ** -0.5)

        for n in range(rn // RET_CC):
            rr = slice(wr.start + n * RET_CC, wr.start + (n + 1) * RET_CC)
            qc = q_ref[rr, :]
            kc = k_ref[rr, :]
            vc = z_ref[rr, COL_V:COL_V + BRANCH_W]
            k_t = jnp.transpose(kc)
            v_stack = _stack_masked(vc).astype(BF16)
            scores = _dot(qc.astype(BF16), _head_columns(k_t).astype(BF16))
            probs = (scores * dcat_ref[...]).astype(BF16)
            inner = _dot(probs, v_stack)
            cross = _dot((qc * xi_ref[...]).astype(BF16), state.astype(BF16))
            ret_ref[rr, :] = inner + cross
            kv = _dot((k_t * zeta_ref[...]).astype(BF16), vc.astype(BF16))
            state = state * cd + kv * bm

        o = ret_ref[wr, :]
        mu = _group_mean(o, avg_ref)
        d = o - mu
        var = _group_mean(d * d, avg_ref)
        o_n = d * lax.rsqrt(var + GN_EPS) * gn_g + gn_b
        gate_r = z_ref[wr, COL_G:COL_G + BRANCH_W]
        y_r = gate_r * jax.nn.sigmoid(gate_r) * o_n
        merged_ref[wr, :] += (_branch_gate(hb, 2, w_gate_ref, b_gate)
                                * _branch_proj(y_r, 2, w_branch_ref))

        u = jax.nn.gelu(z_ref[wr, COL_SU:COL_SU + BRANCH_W])
        v = jax.nn.gelu(z_ref[wr, COL_SV:COL_SV + BRANCH_W])
        v = _layer_norm_rows(v, sln_g, sln_b, LN_EPS)
        pieces = []
        for n in range(rn // SGU_LEN):
            v_stack = _stack_masked(v[n * SGU_LEN:(n + 1) * SGU_LEN, :]).astype(BF16)
            pieces.append(_dot(w_s, v_stack) + sgu_bias_ref[...])
        y_s = u * jnp.concatenate(pieces, axis=0)
        merged = merged_ref[wr, :] + (_branch_gate(hb, 3, w_gate_ref, b_gate)
                                        * _branch_proj(y_s, 3, w_branch_ref))

        out = _dot(merged.astype(BF16), w_out_ref[...])
        o_ref[rows, :] = _layer_norm_rows(ALPHA * h + out, ln_g, ln_b, LN_EPS)
        lo += rn

    cext_ref[0:CONV_HIST, :] = cext_ref[ts:ts + CONV_HIST, :]
    pext_ref[0:POOL_HIST, :] = pext_ref[ts:ts + POOL_HIST, :]
    state_ref[...] = state


def _retention_constants():
    cc = RET_CC
    heads = np.arange(RET_HEADS, dtype=np.float64)
    log_gamma = np.log1p(-(2.0 ** (-5.0 - heads)))
    idx = np.arange(cc, dtype=np.float64)
    diff = idx[:, None] - idx[None, :]
    decay = np.where(diff >= 0,
                     np.exp(log_gamma[:, None, None] * np.maximum(diff, 0.0)), 0.0)
    dcat = np.transpose(decay, (1, 0, 2)).reshape(cc, RET_HEADS * cc)
    zeta = np.exp(log_gamma[:, None] * (cc - 1 - idx)[None, :])
    xi = np.exp(log_gamma[:, None] * (idx + 1.0)[None, :])
    chunk_decay = np.exp(log_gamma * cc)
    zeta_l = np.repeat(zeta.T, RET_DK, axis=1)
    xi_l = np.repeat(xi.T, RET_DK, axis=1)
    head_of = np.arange(BRANCH_W) // RET_DK
    same = head_of[:, None] == head_of[None, :]
    bm = same.astype(np.float64)
    cd = np.where(same, chunk_decay[head_of][:, None], 0.0)
    f32 = lambda a: jnp.asarray(a, F32)
    return (f32(dcat), f32(xi_l), f32(zeta_l.T), f32(cd), f32(bm),
            jnp.asarray(bm / RET_DK, BF16))


def _rope_tables(seq):
    half = RET_DK // 2
    inv = ROPE_BASE ** (-jnp.arange(half, dtype=F32) / half)
    ang = jnp.arange(seq).astype(F32)[:, None] * inv[None, :]
    cos = jnp.cos(ang)
    sin = jnp.sin(ang)
    cos_l = jnp.tile(jnp.concatenate([cos, cos], axis=1), (1, RET_HEADS))
    sin_l = jnp.tile(jnp.concatenate([-sin, sin], axis=1), (1, RET_HEADS))
    return cos_l, sin_l


def _pool_inverse_counts():
    assert POOL_HIST >= max(POOL_WINDOWS)
    win = np.repeat(np.asarray(POOL_WINDOWS, np.float64), POOL_GC)
    head = np.minimum(np.arange(POOL_HIST)[:, None] + 1.0, win[None, :])
    steady = np.broadcast_to(win[None, :], (POOL_HIST, BRANCH_W))
    return jnp.asarray(1.0 / np.concatenate([head, steady], axis=0), F32)


def _mixer_layer(x3d, wts, p, consts, layer, cast_stacks, cast_scales, cast_layer):
    bsz, seq, _ = x3d.shape
    ts = MIX_TS
    nseq = seq // ts
    single = pl.Buffered(1)

    def lspec(shape):
        nd = len(shape)
        return pl.BlockSpec((None,) + shape, lambda b, s: (layer,) + (0,) * nd,
                            pipeline_mode=single)

    def cspec(shape):
        nd = len(shape)
        return pl.BlockSpec(shape, lambda b, s: (0,) * nd, pipeline_mode=single)

    casts = [_cast_io(st, cast_layer, bsz * nseq, lambda b, s: b * nseq + s) for st in cast_stacks]
    tile = pl.BlockSpec((None, ts, D_MODEL), lambda b, s: (b, s, 0))
    table = pl.BlockSpec((ts, BRANCH_W), lambda b, s: (s, 0))
    cos_l, sin_l, icnt, dcat, xi_l, zeta_l, cd, bm, avg = consts
    w_in, w_gate, w_branch, w_out = wts
    in_specs = [
        tile,
        cspec(w_in.shape), cspec(w_gate.shape), cspec(p["b_gate"].shape),
        cspec(w_branch.shape), cspec(w_out.shape),
        lspec((3, BRANCH_W)), lspec((BRANCH_W, BRANCH_W)), cspec(p["pool_scale"].shape),
        cspec(p["gn_g"].shape), cspec(p["gn_b"].shape),
        cspec(p["sln_g"].shape), cspec(p["sln_b"].shape),
        lspec((SGU_LEN, SGU_GROUPS * SGU_LEN)), lspec((SGU_LEN, BRANCH_W)),
        table, table, cspec(icnt.shape),
        cspec(dcat.shape), cspec(xi_l.shape), cspec(zeta_l.shape),
        cspec(cd.shape), cspec(bm.shape), cspec(avg.shape),
        cspec(p["ln_g"].shape), cspec(p["ln_b"].shape),
    ] + [c[0] for c in casts]
    assert len(set(MIX_SUBS)) == 1, "alternating work scratch needs equal sub-tiles"
    work = 2 * MIX_SUBS[0]
    scratch = [
        pltpu.VMEM((work, IN_COLS), F32),
        pltpu.VMEM((CONV_HIST + ts, BRANCH_W), F32),
        pltpu.VMEM((POOL_HIST + ts, BRANCH_W), F32),
        pltpu.VMEM((work, BRANCH_W), F32),
        pltpu.VMEM((work, BRANCH_W), F32),
        pltpu.VMEM((work, BRANCH_W), F32),
        pltpu.VMEM((BRANCH_W, BRANCH_W), F32),
        pltpu.VMEM((work, D_MODEL), F32),
    ]
    outs = pl.pallas_call(
        functools.partial(_mixer_kernel, tuple(cast_scales), layer),
        grid=(bsz, nseq),
        in_specs=in_specs,
        out_specs=[tile] + [c[1] for c in casts],
        out_shape=[jax.ShapeDtypeStruct((bsz, seq, D_MODEL), F32)] + [c[2] for c in casts],
        scratch_shapes=scratch,
        compiler_params=pltpu.CompilerParams(
            dimension_semantics=("arbitrary", "arbitrary"), vmem_limit_bytes=VMEM_LIMIT),
        name="mixer_ln",
    )(x3d, w_in, w_gate, p["b_gate"], w_branch, w_out,
      p["conv_w"], p["pool_bd"], p["pool_scale"], p["gn_g"], p["gn_b"],
      p["sln_g"], p["sln_b"], p["sgu_w"], p["sgu_bias"],
      cos_l, sin_l, icnt, dcat, xi_l, zeta_l, cd, bm, avg, p["ln_g"], p["ln_b"], *cast_stacks)
    return outs[0], outs[1:]


def _prep_mixer_params(conv_w, pool_w, pool_scale, ret_gn_g, ret_gn_b, sgu_ln_g, sgu_ln_b,
                       sgu_w, sgu_b, b_gate, ln2_g, ln2_b):
    depth = conv_w.shape[0]
    groups = pool_w.shape[1]
    eye = jnp.eye(groups, dtype=pool_w.dtype)
    pool_bd = jnp.einsum("lgcd,gh->lgchd", pool_w, eye).reshape(depth, BRANCH_W, BRANCH_W)
    return {
        "b_gate": b_gate,
        "conv_w": conv_w, "pool_bd": pool_bd.astype(BF16), "pool_scale": pool_scale,
        "gn_g": ret_gn_g, "gn_b": ret_gn_b,
        "sln_g": sgu_ln_g, "sln_b": sgu_ln_b,
        "sgu_w": jnp.transpose(sgu_w, (0, 2, 1, 3)).reshape(depth, SGU_LEN, SGU_GROUPS * SGU_LEN),
        "sgu_bias": jnp.repeat(jnp.transpose(sgu_b, (0, 2, 1)), BRANCH_W // SGU_GROUPS, axis=2),
        "ln_g": ln2_g, "ln_b": ln2_b,
    }


def kernel(x, ffn1_w1, ffn1_w2, ln1_g, ln1_b, w_in, conv_w, pool_w, pool_scale, ret_gn_g, ret_gn_b,
           sgu_ln_g, sgu_ln_b, sgu_w, sgu_b, w_branch, w_gate, b_gate, w_out, ln2_g, ln2_b,
           ffn2_w1, ffn2_w2, ln3_g, ln3_b):
    bsz, seq, d = x.shape
    depth = ffn1_w1.shape[0]
    mp = _prep_mixer_params(conv_w, pool_w, pool_scale, ret_gn_g, ret_gn_b, sgu_ln_g, sgu_ln_b,
                            sgu_w, sgu_b, b_gate, ln2_g, ln2_b)
    consts = _rope_tables(seq) + (_pool_inverse_counts(),) + _retention_constants()
    mixer_stacks = (w_in, w_gate, w_branch.reshape(depth, N_BRANCH * BRANCH_W, d), w_out)
    ffn1_stacks = (ffn1_w1, ffn1_w2)
    ffn2_stacks = (ffn2_w1, ffn2_w2)
    ffn_scales = (FFN_W1_SCALE, FFN_W2_SCALE)
    f1 = (_prescale(ffn1_w1[0], FFN_W1_SCALE).astype(BF16),
          _prescale(ffn1_w2[0], FFN_W2_SCALE).astype(BF16))
    x2 = x.reshape(bsz * seq, d)
    for l in range(depth):
        x2, mw = _ffn_layer(x2, f1[0], f1[1], ln1_g, ln1_b, l, mixer_stacks, MIXER_SCALES, l)
        x3, f2 = _mixer_layer(x2.reshape(bsz, seq, d), mw, mp, consts, l,
                              ffn2_stacks, ffn_scales, l)
        last = l + 1 == depth
        x2, f1 = _ffn_layer(x3.reshape(bsz * seq, d), f2[0], f2[1], ln3_g, ln3_b, l,
                            () if last else ffn1_stacks, () if last else ffn_scales, l + 1)
    return x2.reshape(bsz, seq, d)
```

```python
import functools
import math

import jax
import jax.numpy as jnp
import numpy as np
from jax import lax
from jax.experimental import pallas as pl
from jax.experimental.pallas import tpu as pltpu

F32 = jnp.float32
BF16 = jnp.bfloat16

D_MODEL = 1024
DEPTH = 4
CHUNK = 64
BRANCH_W = 256
N_BRANCH = 4
POOL_WINDOWS = (2, 4, 8, 16)
POOL_GC = BRANCH_W // len(POOL_WINDOWS)
RET_HEADS = 4
RET_DK = BRANCH_W // RET_HEADS
ROPE_BASE = 10000.0
SGU_LEN = 128
SGU_GROUPS = 4
D_FF = 2816
ALPHA = (2.0 * DEPTH) ** 0.25
LN_EPS = 1e-5
GN_EPS = 1e-5
IN_COLS = 10 * BRANCH_W

COL_BG, COL_CG, COL_XIN, COL_POOL = 0, 256, 512, 768
COL_Q, COL_K, COL_V, COL_G = 1024, 1280, 1536, 1792
COL_SU, COL_SV = 2048, 2304

FFN_SUBS = (256, 256, 256, 256)
FFN_TM = sum(FFN_SUBS)
MIX_SUBS = (256, 256)
MIX_TS = sum(MIX_SUBS)
RET_CC = 128
CONV_HIST = 8
POOL_HIST = 16
MERGE_COLS = 512
FFN_F_CHUNKS = ((0, 1536), (1536, 2816))
CAST_BLOCKS = 16
VMEM_LIMIT = 56 * 1024 * 1024


def _dot(a, b):
    return jnp.dot(a, b, preferred_element_type=F32)


def _layer_norm_rows(y, g, b, eps):
    mu = jnp.mean(y, axis=-1, keepdims=True)
    d = y - mu
    var = jnp.mean(d * d, axis=-1, keepdims=True)
    return d * lax.rsqrt(var + eps) * g + b


def _cast_io(stack, layer, steps, step_of):
    _, rows, cols = stack.shape
    blocks = math.gcd(steps, CAST_BLOCKS)
    slab = rows // blocks
    assert slab * blocks == rows and slab % 16 == 0
    per = steps // blocks
    in_spec = pl.BlockSpec((None, slab, cols), lambda *g: (layer, step_of(*g) // per, 0))
    out_spec = pl.BlockSpec((slab, cols), lambda *g: (step_of(*g) // per, 0))
    return in_spec, out_spec, jax.ShapeDtypeStruct((rows, cols), BF16)


def _prescale(w, scale):
    if isinstance(scale, tuple):
        split, left, right = scale
        col = lax.broadcasted_iota(jnp.int32, (1, w.shape[-1]), 1)
        return w * jnp.where(col < split, left, right).astype(w.dtype)
    return w if scale == 1.0 else w * scale


def _convert_slabs(src_refs, dst_refs, scales):
    for s_ref, d_ref, scale in zip(src_refs, dst_refs, scales, strict=True):
        d_ref[...] = _prescale(s_ref[...], scale).astype(BF16)


FFN_W1_SCALE = (D_FF, 0.5, 1.0)
FFN_W2_SCALE = 0.5
MIXER_SCALES = (1.0, 0.5, 0.5, 1.0)


def _ffn_kernel(cast_scales, layer, x_ref, w1_ref, w2_ref, g_ref, b_ref, *refs):
    n_cast = len(cast_scales)
    cast_src, o_ref, cast_dst = refs[:n_cast], refs[n_cast], refs[n_cast + 1:]
    _convert_slabs(cast_src, cast_dst, cast_scales)
    ln_g = g_ref[layer:layer + 1, :]
    ln_b = b_ref[layer:layer + 1, :]
    lo = 0
    for rows_n in FFN_SUBS:
        rows = slice(lo, lo + rows_n)
        lo += rows_n
        x = x_ref[rows, :]
        xb = x.astype(BF16)
        ffn = None
        for c_lo, c_hi in FFN_F_CHUNKS:
            half_gate = _dot(xb, w1_ref[:, c_lo:c_hi])
            up = _dot(xb, w1_ref[:, D_FF + c_lo:D_FF + c_hi])
            act = (half_gate * (1.0 + jnp.tanh(half_gate)) * up).astype(BF16)
            part = _dot(act, w2_ref[c_lo:c_hi, :])
            ffn = part if ffn is None else ffn + part
        y = ALPHA * x + ffn
        o_ref[rows, :] = _layer_norm_rows(y, ln_g, ln_b, LN_EPS)


def _ffn_layer(x2d, w1, w2, g_all, b_all, layer, cast_stacks, cast_scales, cast_layer):
    m = x2d.shape[0]
    steps = m // FFN_TM
    whole = lambda i: (0, 0)
    single = pl.Buffered(1)
    casts = [_cast_io(s, cast_layer, steps, lambda i: i) for s in cast_stacks]
    outs = pl.pallas_call(
        functools.partial(_ffn_kernel, tuple(cast_scales), layer),
        grid=(steps,),
        in_specs=[
            pl.BlockSpec((FFN_TM, D_MODEL), lambda i: (i, 0)),
            pl.BlockSpec((D_MODEL, 2 * D_FF), whole, pipeline_mode=single),
            pl.BlockSpec((D_FF, D_MODEL), whole, pipeline_mode=single),
            pl.BlockSpec(g_all.shape, whole, pipeline_mode=single),
            pl.BlockSpec(b_all.shape, whole, pipeline_mode=single),
        ] + [c[0] for c in casts],
        out_specs=[pl.BlockSpec((FFN_TM, D_MODEL), lambda i: (i, 0))] + [c[1] for c in casts],
        out_shape=[jax.ShapeDtypeStruct((m, D_MODEL), F32)] + [c[2] for c in casts],
        compiler_params=pltpu.CompilerParams(
            dimension_semantics=("arbitrary",), vmem_limit_bytes=VMEM_LIMIT),
        name="ffn_ln",
    )(x2d, w1, w2, g_all, b_all, *cast_stacks)
    return outs[0], outs[1:]


def _lane_group_mask(shape, group):
    lane = lax.broadcasted_iota(jnp.int32, shape, 1)
    return (lane // RET_DK) == group


def _stack_masked(x):
    return jnp.concatenate(
        [jnp.where(_lane_group_mask(x.shape, g), x, 0.0) for g in range(RET_HEADS)], axis=0)


def _head_columns(x_t):
    rows, cc = x_t.shape
    blocks = []
    for g in range(RET_HEADS):
        lo, hi = g * RET_DK, (g + 1) * RET_DK
        parts = [jnp.zeros((lo, cc), x_t.dtype)] if lo else []
        parts.append(x_t[lo:hi, :])
        if hi < rows:
            parts.append(jnp.zeros((rows - hi, cc), x_t.dtype))
        blocks.append(jnp.concatenate(parts, axis=0))
    return jnp.concatenate(blocks, axis=1)


def _group_mean(x, avg_ref):
    hi = x.astype(BF16)
    lo = (x - hi.astype(F32)).astype(BF16)
    return _dot(hi, avg_ref[...]) + _dot(lo, avg_ref[...])


def _gated_branch(y, hb, n, w_gate_ref, half_b_gate, w_branch_ref):
    yb = y.astype(BF16)
    parts = []
    for lo in range(0, D_MODEL, MERGE_COLS):
        gcols = slice(n * D_MODEL + lo, n * D_MODEL + lo + MERGE_COLS)
        gate2 = 1.0 + jnp.tanh(_dot(hb, w_gate_ref[:, gcols]) + half_b_gate[:, gcols])
        proj = _dot(yb, w_branch_ref[n * BRANCH_W:(n + 1) * BRANCH_W, lo:lo + MERGE_COLS])
        parts.append(gate2 * proj)
    return jnp.concatenate(parts, axis=1)


def _mixer_kernel(cast_scales, layer,
                  h_ref, w_in_ref, w_gate_ref, b_gate_ref, w_branch_ref, w_out_ref,
                  conv_w_ref, pool_bd_ref, pool_scale_ref, gn_g_ref, gn_b_ref,
                  sln_g_ref, sln_b_ref, sgu_w_ref, sgu_bias_ref,
                  cos_ref, sin_ref, icnt_ref, dcat_ref, xi_ref, zeta_ref, cd_ref, bm_ref, avg_ref,
                  ln_g_ref, ln_b_ref, *refs):
    n_cast = len(cast_scales)
    cast_src, o_ref, cast_dst = refs[:n_cast], refs[n_cast], refs[n_cast + 1:2 * n_cast + 1]
    (z_ref, cext_ref, pext_ref, q_ref, k_ref, ret_ref, state_ref,
     merged_ref) = refs[2 * n_cast + 1:]
    ts = h_ref.shape[0]
    s = pl.program_id(1)

    _convert_slabs(cast_src, cast_dst, cast_scales)

    @pl.when(s == 0)
    def _():
        cext_ref[0:CONV_HIST, :] = jnp.zeros((CONV_HIST, BRANCH_W), F32)
        pext_ref[0:POOL_HIST, :] = jnp.zeros((POOL_HIST, BRANCH_W), F32)
        state_ref[...] = jnp.zeros_like(state_ref)

    wi = lax.broadcasted_iota(jnp.int32, (SGU_LEN, SGU_GROUPS * SGU_LEN), 0)
    wj = lax.broadcasted_iota(jnp.int32, (SGU_LEN, SGU_GROUPS * SGU_LEN), 1) % SGU_LEN
    w_s = jnp.where((wj // CHUNK) <= (wi // CHUNK), sgu_w_ref[...], 0.0).astype(BF16)
    cw = conv_w_ref[...]
    cd = cd_ref[...]
    bm = bm_ref[...]
    state = state_ref[...]
    layer_row = lambda ref: ref[layer:layer + 1, :]
    b_gate, pool_scale = 0.5 * layer_row(b_gate_ref), layer_row(pool_scale_ref)
    gn_g, gn_b = layer_row(gn_g_ref), layer_row(gn_b_ref)
    sln_g, sln_b = layer_row(sln_g_ref), layer_row(sln_b_ref)
    ln_g, ln_b = layer_row(ln_g_ref), layer_row(ln_b_ref)

    lo = 0
    for j, rn in enumerate(MIX_SUBS):
        rows = slice(lo, lo + rn)
        wr = slice((j % 2) * rn, (j % 2 + 1) * rn)
        h = h_ref[rows, :]
        hb = h.astype(BF16)
        z_ref[wr, :] = _dot(hb, w_in_ref[...])

        c = z_ref[wr, COL_CG:COL_CG + BRANCH_W] * z_ref[wr, COL_XIN:COL_XIN + BRANCH_W]
        at = CONV_HIST + lo
        cext_ref[at:at + rn, :] = c
        conv = (cw[0:1, :] * cext_ref[at - 2:at - 2 + rn, :]
                + cw[1:2, :] * cext_ref[at - 1:at - 1 + rn, :]
                + cw[2:3, :] * c)
        y_a = z_ref[wr, COL_BG:COL_BG + BRANCH_W] * conv
        merged_ref[wr, :] = _gated_branch(y_a, hb, 0, w_gate_ref, b_gate, w_branch_ref)

        zp = z_ref[wr, COL_POOL:COL_POOL + BRANCH_W]
        pext_ref[POOL_HIST + lo:POOL_HIST + lo + rn, :] = zp
        half = BRANCH_W // 2
        low_group = lax.broadcasted_iota(jnp.int32, (rn, half), 1) < POOL_GC
        ext = pext_ref[lo:lo + POOL_HIST + rn, 0:half]
        s2 = ext + pltpu.roll(ext, 1, 0)
        s4 = s2 + pltpu.roll(s2, 2, 0)
        wsum_lo = jnp.where(low_group, s2[POOL_HIST:], s4[POOL_HIST:])
        ext = pext_ref[lo:lo + POOL_HIST + rn, half:BRANCH_W]
        s2 = ext + pltpu.roll(ext, 1, 0)
        s4 = s2 + pltpu.roll(s2, 2, 0)
        s8 = s4 + pltpu.roll(s4, 4, 0)
        s16 = s8 + pltpu.roll(s8, 8, 0)
        wsum_hi = jnp.where(low_group, s8[POOL_HIST:], s16[POOL_HIST:])
        wsum = jnp.concatenate([wsum_lo, wsum_hi], axis=1)
        lane = lax.broadcasted_iota(jnp.int32, (rn, BRANCH_W), 1)
        mixed = wsum * icnt_ref[POOL_HIST:POOL_HIST + 1, :] - zp
        if j == 0:
            inv_cnt = jnp.where(s == 0, icnt_ref[0:POOL_HIST, :], icnt_ref[POOL_HIST:, :])
            mixed = jnp.concatenate(
                [wsum[0:POOL_HIST, :] * inv_cnt - zp[0:POOL_HIST, :], mixed[POOL_HIST:, :]], axis=0)
        y_p = _dot(mixed.astype(BF16), pool_bd_ref[...]) * pool_scale
        merged_ref[wr, :] += _gated_branch(y_p, hb, 1, w_gate_ref, b_gate, w_branch_ref)

        cos = cos_ref[rows, :]
        sin = sin_ref[rows, :]
        first_half = (lane % RET_DK) < (RET_DK // 2)

        def rope(x):
            swapped = jnp.where(first_half, pltpu.roll(x, BRANCH_W - RET_DK // 2, 1),
                                pltpu.roll(x, RET_DK // 2, 1))
            return x * cos + swapped * sin

        q_ref[wr, :] = rope(z_ref[wr, COL_Q:COL_Q + BRANCH_W])
        k_ref[wr, :] = rope(z_ref[wr, COL_K:COL_K + BRANCH_W]) * (RET_DK ** -0.5)

        for n in range(rn // RET_CC):
            rr = slice(wr.start + n * RET_CC, wr.start + (n + 1) * RET_CC)
            qc = q_ref[rr, :]
            kc = k_ref[rr, :]
            vc = z_ref[rr, COL_V:COL_V + BRANCH_W]
            k_t = jnp.transpose(kc)
            v_stack = _stack_masked(vc).astype(BF16)
            scores = _dot(qc.astype(BF16), _head_columns(k_t).astype(BF16))
            probs = (scores * dcat_ref[...]).astype(BF16)
            inner = _dot(probs, v_stack)
            cross = _dot((qc * xi_ref[...]).astype(BF16), state.astype(BF16))
            ret_ref[rr, :] = inner + cross
            kv = _dot((k_t * zeta_ref[...]).astype(BF16), vc.astype(BF16))
            state = state * cd + kv * bm

        o = ret_ref[wr, :]
        mu = _group_mean(o, avg_ref)
        d = o - mu
        var = _group_mean(d * d, avg_ref)
        o_n = d * lax.rsqrt(var + GN_EPS) * gn_g + gn_b
        gate_r = z_ref[wr, COL_G:COL_G + BRANCH_W]
        y_r = gate_r * jax.nn.sigmoid(gate_r) * o_n
        merged_ref[wr, :] += _gated_branch(y_r, hb, 2, w_gate_ref, b_gate, w_branch_ref)

        u = jax.nn.gelu(z_ref[wr, COL_SU:COL_SU + BRANCH_W])
        v = jax.nn.gelu(z_ref[wr, COL_SV:COL_SV + BRANCH_W])
        v = _layer_norm_rows(v, sln_g, sln_b, LN_EPS)
        pieces = []
        for n in range(rn // SGU_LEN):
            v_stack = _stack_masked(v[n * SGU_LEN:(n + 1) * SGU_LEN, :]).astype(BF16)
            pieces.append(_dot(w_s, v_stack) + sgu_bias_ref[...])
        y_s = u * jnp.concatenate(pieces, axis=0)
        merged = merged_ref[wr, :] + _gated_branch(y_s, hb, 3, w_gate_ref, b_gate, w_branch_ref)

        out = _dot(merged.astype(BF16), w_out_ref[...])
        o_ref[rows, :] = _layer_norm_rows(ALPHA * h + out, ln_g, ln_b, LN_EPS)
        lo += rn

    cext_ref[0:CONV_HIST, :] = cext_ref[ts:ts + CONV_HIST, :]
    pext_ref[0:POOL_HIST, :] = pext_ref[ts:ts + POOL_HIST, :]
    state_ref[...] = state


def _retention_constants():
    cc = RET_CC
    heads = np.arange(RET_HEADS, dtype=np.float64)
    log_gamma = np.log1p(-(2.0 ** (-5.0 - heads)))
    idx = np.arange(cc, dtype=np.float64)
    diff = idx[:, None] - idx[None, :]
    decay = np.where(diff >= 0,
                     np.exp(log_gamma[:, None, None] * np.maximum(diff, 0.0)), 0.0)
    dcat = np.transpose(decay, (1, 0, 2)).reshape(cc, RET_HEADS * cc)
    zeta = np.exp(log_gamma[:, None] * (cc - 1 - idx)[None, :])
    xi = np.exp(log_gamma[:, None] * (idx + 1.0)[None, :])
    chunk_decay = np.exp(log_gamma * cc)
    zeta_l = np.repeat(zeta.T, RET_DK, axis=1)
    xi_l = np.repeat(xi.T, RET_DK, axis=1)
    head_of = np.arange(BRANCH_W) // RET_DK
    same = head_of[:, None] == head_of[None, :]
    bm = same.astype(np.float64)
    cd = np.where(same, chunk_decay[head_of][:, None], 0.0)
    f32 = lambda a: jnp.asarray(a, F32)
    return (f32(dcat), f32(xi_l), f32(zeta_l.T), f32(cd), f32(bm),
            jnp.asarray(bm / RET_DK, BF16))


def _rope_tables(seq):
    half = RET_DK // 2
    inv = ROPE_BASE ** (-jnp.arange(half, dtype=F32) / half)
    ang = jnp.arange(seq).astype(F32)[:, None] * inv[None, :]
    cos = jnp.cos(ang)
    sin = jnp.sin(ang)
    cos_l = jnp.tile(jnp.concatenate([cos, cos], axis=1), (1, RET_HEADS))
    sin_l = jnp.tile(jnp.concatenate([-sin, sin], axis=1), (1, RET_HEADS))
    return cos_l, sin_l


def _pool_inverse_counts():
    assert POOL_HIST >= max(POOL_WINDOWS)
    win = np.repeat(np.asarray(POOL_WINDOWS, np.float64), POOL_GC)
    head = np.minimum(np.arange(POOL_HIST)[:, None] + 1.0, win[None, :])
    steady = np.broadcast_to(win[None, :], (POOL_HIST, BRANCH_W))
    return jnp.asarray(1.0 / np.concatenate([head, steady], axis=0), F32)


def _mixer_layer(x3d, wts, p, consts, layer, cast_stacks, cast_scales, cast_layer):
    bsz, seq, _ = x3d.shape
    ts = MIX_TS
    nseq = seq // ts
    single = pl.Buffered(1)

    def lspec(shape):
        nd = len(shape)
        return pl.BlockSpec((None,) + shape, lambda b, s: (layer,) + (0,) * nd,
                            pipeline_mode=single)

    def cspec(shape):
        nd = len(shape)
        return pl.BlockSpec(shape, lambda b, s: (0,) * nd, pipeline_mode=single)

    casts = [_cast_io(st, cast_layer, bsz * nseq, lambda b, s: b * nseq + s) for st in cast_stacks]
    tile = pl.BlockSpec((None, ts, D_MODEL), lambda b, s: (b, s, 0))
    table = pl.BlockSpec((ts, BRANCH_W), lambda b, s: (s, 0))
    cos_l, sin_l, icnt, dcat, xi_l, zeta_l, cd, bm, avg = consts
    w_in, w_gate, w_branch, w_out = wts
    in_specs = [
        tile,
        cspec(w_in.shape), cspec(w_gate.shape), cspec(p["b_gate"].shape),
        cspec(w_branch.shape), cspec(w_out.shape),
        lspec((3, BRANCH_W)), lspec((BRANCH_W, BRANCH_W)), cspec(p["pool_scale"].shape),
        cspec(p["gn_g"].shape), cspec(p["gn_b"].shape),
        cspec(p["sln_g"].shape), cspec(p["sln_b"].shape),
        lspec((SGU_LEN, SGU_GROUPS * SGU_LEN)), lspec((SGU_LEN, BRANCH_W)),
        table, table, cspec(icnt.shape),
        cspec(dcat.shape), cspec(xi_l.shape), cspec(zeta_l.shape),
        cspec(cd.shape), cspec(bm.shape), cspec(avg.shape),
        cspec(p["ln_g"].shape), cspec(p["ln_b"].shape),
    ] + [c[0] for c in casts]
    assert len(set(MIX_SUBS)) == 1, "alternating work scratch needs equal sub-tiles"
    work = 2 * MIX_SUBS[0]
    scratch = [
        pltpu.VMEM((work, IN_COLS), F32),
        pltpu.VMEM((CONV_HIST + ts, BRANCH_W), F32),
        pltpu.VMEM((POOL_HIST + ts, BRANCH_W), F32),
        pltpu.VMEM((work, BRANCH_W), F32),
        pltpu.VMEM((work, BRANCH_W), F32),
        pltpu.VMEM((work, BRANCH_W), F32),
        pltpu.VMEM((BRANCH_W, BRANCH_W), F32),
        pltpu.VMEM((work, D_MODEL), F32),
    ]
    outs = pl.pallas_call(
        functools.partial(_mixer_kernel, tuple(cast_scales), layer),
        grid=(bsz, nseq),
        in_specs=in_specs,
        out_specs=[tile] + [c[1] for c in casts],
        out_shape=[jax.ShapeDtypeStruct((bsz, seq, D_MODEL), F32)] + [c[2] for c in casts],
        scratch_shapes=scratch,
        compiler_params=pltpu.CompilerParams(
            dimension_semantics=("arbitrary", "arbitrary"), vmem_limit_bytes=VMEM_LIMIT),
        name="mixer_ln",
    )(x3d, w_in, w_gate, p["b_gate"], w_branch, w_out,
      p["conv_w"], p["pool_bd"], p["pool_scale"], p["gn_g"], p["gn_b"],
      p["sln_g"], p["sln_b"], p["sgu_w"], p["sgu_bias"],
      cos_l, sin_l, icnt, dcat, xi_l, zeta_l, cd, bm, avg, p["ln_g"], p["ln_b"], *cast_stacks)
    return outs[0], outs[1:]


def _prep_mixer_params(conv_w, pool_w, pool_scale, ret_gn_g, ret_gn_b, sgu_ln_g, sgu_ln_b,
                       sgu_w, sgu_b, b_gate, ln2_g, ln2_b):
    depth = conv_w.shape[0]
    groups = pool_w.shape[1]
    eye = jnp.eye(groups, dtype=pool_w.dtype)
    pool_bd = jnp.einsum("lgcd,gh->lgchd", pool_w, eye).reshape(depth, BRANCH_W, BRANCH_W)
    return {
        "b_gate": b_gate,
        "conv_w": conv_w, "pool_bd": pool_bd.astype(BF16), "pool_scale": pool_scale,
        "gn_g": ret_gn_g, "gn_b": ret_gn_b,
        "sln_g": sgu_ln_g, "sln_b": sgu_ln_b,
        "sgu_w": jnp.transpose(sgu_w, (0, 2, 1, 3)).reshape(depth, SGU_LEN, SGU_GROUPS * SGU_LEN),
        "sgu_bias": jnp.repeat(jnp.transpose(sgu_b, (0, 2, 1)), BRANCH_W // SGU_GROUPS, axis=2),
        "ln_g": ln2_g, "ln_b": ln2_b,
    }


def kernel(x, ffn1_w1, ffn1_w2, ln1_g, ln1_b, w_in, conv_w, pool_w, pool_scale, ret_gn_g, ret_gn_b,
           sgu_ln_g, sgu_ln_b, sgu_w, sgu_b, w_branch, w_gate, b_gate, w_out, ln2_g, ln2_b,
           ffn2_w1, ffn2_w2, ln3_g, ln3_b):
    bsz, seq, d = x.shape
    depth = ffn1_w1.shape[0]
    mp = _prep_mixer_params(conv_w, pool_w, pool_scale, ret_gn_g, ret_gn_b, sgu_ln_g, sgu_ln_b,
                            sgu_w, sgu_b, b_gate, ln2_g, ln2_b)
    consts = _rope_tables(seq) + (_pool_inverse_counts(),) + _retention_constants()
    mixer_stacks = (w_in, w_gate, w_branch.reshape(depth, N_BRANCH * BRANCH_W, d), w_out)
    ffn1_stacks = (ffn1_w1, ffn1_w2)
    ffn2_stacks = (ffn2_w1, ffn2_w2)
    ffn_scales = (FFN_W1_SCALE, FFN_W2_SCALE)
    f1 = (_prescale(ffn1_w1[0], FFN_W1_SCALE).astype(BF16),
          _prescale(ffn1_w2[0], FFN_W2_SCALE).astype(BF16))
    x2 = x.reshape(bsz * seq, d)
    for l in range(depth):
        x2, mw = _ffn_layer(x2, f1[0], f1[1], ln1_g, ln1_b, l, mixer_stacks, MIXER_SCALES, l)
        x3, f2 = _mixer_layer(x2.reshape(bsz, seq, d), mw, mp, consts, l,
                              ffn2_stacks, ffn_scales, l)
        last = l + 1 == depth
        x2, f1 = _ffn_layer(x3.reshape(bsz * seq, d), f2[0], f2[1], ln3_g, ln3_b, l,
                            () if last else ffn1_stacks, () if last else ffn_scales, l + 1)
    return x2.reshape(bsz, seq, d)
```

```python
import functools
import math

import jax
import jax.numpy as jnp
import numpy as np
from jax import lax
from jax.experimental import pallas as pl
from jax.experimental.pallas import tpu as pltpu

F32 = jnp.float32
BF16 = jnp.bfloat16

D_MODEL = 1024
DEPTH = 4
CHUNK = 64
BRANCH_W = 256
N_BRANCH = 4
POOL_WINDOWS = (2, 4, 8, 16)
POOL_GC = BRANCH_W // len(POOL_WINDOWS)
RET_HEADS = 4
RET_DK = BRANCH_W // RET_HEADS
ROPE_BASE = 10000.0
SGU_LEN = 128
SGU_GROUPS = 4
D_FF = 2816
ALPHA = (2.0 * DEPTH) ** 0.25
LN_EPS = 1e-5
GN_EPS = 1e-5
IN_COLS = 10 * BRANCH_W

COL_BG, COL_CG, COL_XIN, COL_POOL = 0, 256, 512, 768
COL_Q, COL_K, COL_V, COL_G = 1024, 1280, 1536, 1792
COL_SU, COL_SV = 2048, 2304

FFN_SUBS = (256, 256, 256, 256)
FFN_TM = sum(FFN_SUBS)
MIX_SUBS = (256, 256)
MIX_TS = sum(MIX_SUBS)
RET_CC = 128
CONV_HIST = 8
ROPE_LANES = 128
POOL_HIST = 16
MERGE_COLS = 512
FFN_F_CHUNKS = ((0, 1536), (1536, 2816))
CAST_BLOCKS = 16
VMEM_LIMIT = 56 * 1024 * 1024


def _dot(a, b):
    return jnp.dot(a, b, preferred_element_type=F32)


def _layer_norm_rows(y, g, b, eps):
    mu = jnp.mean(y, axis=-1, keepdims=True)
    d = y - mu
    var = jnp.mean(d * d, axis=-1, keepdims=True)
    return d * lax.rsqrt(var + eps) * g + b


def _cast_io(stack, layer, steps, step_of):
    _, rows, cols = stack.shape
    blocks = math.gcd(steps, CAST_BLOCKS)
    slab = rows // blocks
    assert slab * blocks == rows and slab % 16 == 0
    per = steps // blocks
    in_spec = pl.BlockSpec((None, slab, cols), lambda *g: (layer, step_of(*g) // per, 0))
    out_spec = pl.BlockSpec((slab, cols), lambda *g: (step_of(*g) // per, 0))
    return in_spec, out_spec, jax.ShapeDtypeStruct((rows, cols), BF16)


def _prescale(w, scale):
    if isinstance(scale, tuple):
        split, left, right = scale
        col = lax.broadcasted_iota(jnp.int32, (1, w.shape[-1]), 1)
        return w * jnp.where(col < split, left, right).astype(w.dtype)
    return w if scale == 1.0 else w * scale


def _convert_slabs(src_refs, dst_refs, scales):
    for s_ref, d_ref, scale in zip(src_refs, dst_refs, scales, strict=True):
        d_ref[...] = _prescale(s_ref[...], scale).astype(BF16)


FFN_W1_SCALE = (D_FF, 0.5, 1.0)
FFN_W2_SCALE = 0.5
MIXER_SCALES = (1.0, 0.5, 0.5, 1.0)


def _ffn_kernel(cast_scales, layer, x_ref, w1_ref, w2_ref, g_ref, b_ref, *refs):
    n_cast = len(cast_scales)
    cast_src, o_ref, cast_dst = refs[:n_cast], refs[n_cast], refs[n_cast + 1:]
    _convert_slabs(cast_src, cast_dst, cast_scales)
    ln_g = g_ref[layer:layer + 1, :]
    ln_b = b_ref[layer:layer + 1, :]
    lo = 0
    for rows_n in FFN_SUBS:
        rows = slice(lo, lo + rows_n)
        lo += rows_n
        x = x_ref[rows, :]
        xb = x.astype(BF16)
        ffn = None
        for c_lo, c_hi in FFN_F_CHUNKS:
            half_gate = _dot(xb, w1_ref[:, c_lo:c_hi])
            up = _dot(xb, w1_ref[:, D_FF + c_lo:D_FF + c_hi])
            act = (half_gate * (1.0 + jnp.tanh(half_gate)) * up).astype(BF16)
            part = _dot(act, w2_ref[c_lo:c_hi, :])
            ffn = part if ffn is None else ffn + part
        y = ALPHA * x + ffn
        o_ref[rows, :] = _layer_norm_rows(y, ln_g, ln_b, LN_EPS)


def _ffn_layer(x2d, w1, w2, g_all, b_all, layer, cast_stacks, cast_scales, cast_layer):
    m = x2d.shape[0]
    steps = m // FFN_TM
    whole = lambda i: (0, 0)
    single = pl.Buffered(1)
    casts = [_cast_io(s, cast_layer, steps, lambda i: i) for s in cast_stacks]
    outs = pl.pallas_call(
        functools.partial(_ffn_kernel, tuple(cast_scales), layer),
        grid=(steps,),
        in_specs=[
            pl.BlockSpec((FFN_TM, D_MODEL), lambda i: (i, 0)),
            pl.BlockSpec((D_MODEL, 2 * D_FF), whole, pipeline_mode=single),
            pl.BlockSpec((D_FF, D_MODEL), whole, pipeline_mode=single),
            pl.BlockSpec(g_all.shape, whole, pipeline_mode=single),
            pl.BlockSpec(b_all.shape, whole, pipeline_mode=single),
        ] + [c[0] for c in casts],
        out_specs=[pl.BlockSpec((FFN_TM, D_MODEL), lambda i: (i, 0))] + [c[1] for c in casts],
        out_shape=[jax.ShapeDtypeStruct((m, D_MODEL), F32)] + [c[2] for c in casts],
        compiler_params=pltpu.CompilerParams(
            dimension_semantics=("arbitrary",), vmem_limit_bytes=VMEM_LIMIT),
        name="ffn_ln",
    )(x2d, w1, w2, g_all, b_all, *cast_stacks)
    return outs[0], outs[1:]


def _lane_group_mask(shape, group):
    lane = lax.broadcasted_iota(jnp.int32, shape, 1)
    return (lane // RET_DK) == group


def _stack_masked(x):
    return jnp.concatenate(
        [jnp.where(_lane_group_mask(x.shape, g), x, 0.0) for g in range(RET_HEADS)], axis=0)


def _head_columns(x_t):
    rows, cc = x_t.shape
    blocks = []
    for g in range(RET_HEADS):
        lo, hi = g * RET_DK, (g + 1) * RET_DK
        parts = [jnp.zeros((lo, cc), x_t.dtype)] if lo else []
        parts.append(x_t[lo:hi, :])
        if hi < rows:
            parts.append(jnp.zeros((rows - hi, cc), x_t.dtype))
        blocks.append(jnp.concatenate(parts, axis=0))
    return jnp.concatenate(blocks, axis=1)


def _group_mean(x, avg_ref):
    hi = x.astype(BF16)
    lo = (x - hi.astype(F32)).astype(BF16)
    return _dot(jnp.concatenate([hi, lo], axis=1), avg_ref[...])


def _add_gated_branch(acc_ref, rows, y, hb, n, w_gate_ref, half_b_gate, w_branch_ref):
    yb = y.astype(BF16)
    for lo in range(0, D_MODEL, MERGE_COLS):
        cols = slice(lo, lo + MERGE_COLS)
        gcols = slice(n * D_MODEL + lo, n * D_MODEL + lo + MERGE_COLS)
        gate2 = 1.0 + jnp.tanh(_dot(hb, w_gate_ref[:, gcols]) + half_b_gate[:, gcols])
        term = gate2 * _dot(yb, w_branch_ref[n * BRANCH_W:(n + 1) * BRANCH_W, cols])
        if n == 0:
            acc_ref[rows, cols] = term
        else:
            acc_ref[rows, cols] += term


def _mixer_kernel(cast_scales, layer,
                  h_ref, w_in_ref, w_gate_ref, b_gate_ref, w_branch_ref, w_out_ref,
                  conv_w_ref, pool_bd_ref, pool_scale_ref, gn_g_ref, gn_b_ref,
                  sln_g_ref, sln_b_ref, sgu_w_ref, sgu_bias_ref,
                  cos_ref, sin_ref, icnt_ref, dcat_ref, xi_ref, zeta_ref, cd_ref, bm_ref, avg_ref,
                  ln_g_ref, ln_b_ref, *refs):
    n_cast = len(cast_scales)
    cast_src, o_ref, cast_dst = refs[:n_cast], refs[n_cast], refs[n_cast + 1:2 * n_cast + 1]
    (z_ref, cext_ref, pext_ref, q_ref, k_ref, ret_ref, state_ref,
     merged_ref) = refs[2 * n_cast + 1:]
    ts = h_ref.shape[0]
    s = pl.program_id(1)

    _convert_slabs(cast_src, cast_dst, cast_scales)

    @pl.when(s == 0)
    def _():
        cext_ref[0:CONV_HIST, :] = jnp.zeros((CONV_HIST, BRANCH_W), F32)
        pext_ref[0:POOL_HIST, :] = jnp.zeros((POOL_HIST, BRANCH_W), F32)
        state_ref[...] = jnp.zeros_like(state_ref)

    wi = lax.broadcasted_iota(jnp.int32, (SGU_LEN, SGU_GROUPS * SGU_LEN), 0)
    wj = lax.broadcasted_iota(jnp.int32, (SGU_LEN, SGU_GROUPS * SGU_LEN), 1) % SGU_LEN
    w_s = jnp.where((wj // CHUNK) <= (wi // CHUNK), sgu_w_ref[...], 0.0).astype(BF16)
    cw = conv_w_ref[...]
    cd = cd_ref[...]
    bm = bm_ref[...]
    state = state_ref[...]
    layer_row = lambda ref: ref[layer:layer + 1, :]
    b_gate, pool_scale = 0.5 * layer_row(b_gate_ref), layer_row(pool_scale_ref)
    gn_g, gn_b = layer_row(gn_g_ref), layer_row(gn_b_ref)
    sln_g, sln_b = layer_row(sln_g_ref), layer_row(sln_b_ref)
    ln_g, ln_b = layer_row(ln_g_ref), layer_row(ln_b_ref)

    lo = 0
    for j, rn in enumerate(MIX_SUBS):
        rows = slice(lo, lo + rn)
        wr = slice((j % 2) * rn, (j % 2 + 1) * rn)
        h = h_ref[rows, :]
        hb = h.astype(BF16)
        z_ref[wr, :] = _dot(hb, w_in_ref[...])

        c = z_ref[wr, COL_CG:COL_CG + BRANCH_W] * z_ref[wr, COL_XIN:COL_XIN + BRANCH_W]
        at = CONV_HIST + lo
        cext_ref[at:at + rn, :] = c
        conv = (cw[0:1, :] * cext_ref[at - 2:at - 2 + rn, :]
                + cw[1:2, :] * cext_ref[at - 1:at - 1 + rn, :]
                + cw[2:3, :] * c)
        y_a = z_ref[wr, COL_BG:COL_BG + BRANCH_W] * conv
        _add_gated_branch(merged_ref, wr, y_a, hb, 0, w_gate_ref, b_gate, w_branch_ref)

        zp = z_ref[wr, COL_POOL:COL_POOL + BRANCH_W]
        pext_ref[POOL_HIST + lo:POOL_HIST + lo + rn, :] = zp
        half = BRANCH_W // 2
        low_group = lax.broadcasted_iota(jnp.int32, (rn, half), 1) < POOL_GC
        ext = pext_ref[lo:lo + POOL_HIST + rn, 0:half]
        s2 = ext + pltpu.roll(ext, 1, 0)
        s4 = s2 + pltpu.roll(s2, 2, 0)
        wsum_lo = jnp.where(low_group, s2[POOL_HIST:], s4[POOL_HIST:])
        ext = pext_ref[lo:lo + POOL_HIST + rn, half:BRANCH_W]
        s2 = ext + pltpu.roll(ext, 1, 0)
        s4 = s2 + pltpu.roll(s2, 2, 0)
        s8 = s4 + pltpu.roll(s4, 4, 0)
        s16 = s8 + pltpu.roll(s8, 8, 0)
        wsum_hi = jnp.where(low_group, s8[POOL_HIST:], s16[POOL_HIST:])
        wsum = jnp.concatenate([wsum_lo, wsum_hi], axis=1)
        lane = lax.broadcasted_iota(jnp.int32, (rn, BRANCH_W), 1)
        mixed = wsum * icnt_ref[POOL_HIST:POOL_HIST + 1, :] - zp
        if j == 0:
            inv_cnt = jnp.where(s == 0, icnt_ref[0:POOL_HIST, :], icnt_ref[POOL_HIST:, :])
            mixed = jnp.concatenate(
                [wsum[0:POOL_HIST, :] * inv_cnt - zp[0:POOL_HIST, :], mixed[POOL_HIST:, :]], axis=0)
        y_p = _dot(mixed.astype(BF16), pool_bd_ref[...]) * pool_scale
        _add_gated_branch(merged_ref, wr, y_p, hb, 1, w_gate_ref, b_gate, w_branch_ref)

        cos = jnp.concatenate([cos_ref[rows, :]] * (BRANCH_W // ROPE_LANES), axis=1)
        sin = jnp.concatenate([sin_ref[rows, :]] * (BRANCH_W // ROPE_LANES), axis=1)
        first_half = (lane % RET_DK) < (RET_DK // 2)

        def rope(x):
            swapped = jnp.where(first_half, pltpu.roll(x, BRANCH_W - RET_DK // 2, 1),
                                pltpu.roll(x, RET_DK // 2, 1))
            return x * cos + swapped * sin

        q_ref[wr, :] = rope(z_ref[wr, COL_Q:COL_Q + BRANCH_W])
        k_ref[wr, :] = rope(z_ref[wr, COL_K:COL_K + BRANCH_W]) * (RET_DK ** -0.5)

        for n in range(rn // RET_CC):
            rr = slice(wr.start + n * RET_CC, wr.start + (n + 1) * RET_CC)
            qc = q_ref[rr, :]
            kc = k_ref[rr, :]
            vc = z_ref[rr, COL_V:COL_V + BRANCH_W]
            k_t = jnp.transpose(kc)
            v_stack = _stack_masked(vc).astype(BF16)
            scores = _dot(qc.astype(BF16), _head_columns(k_t).astype(BF16))
            probs = (scores * dcat_ref[...]).astype(BF16)
            inner = _dot(probs, v_stack)
            cross = _dot((qc * xi_ref[...]).astype(BF16), state.astype(BF16))
            ret_ref[rr, :] = inner + cross
            kv = _dot((k_t * zeta_ref[...]).astype(BF16), vc.astype(BF16))
            state = state * cd + kv * bm

        o = ret_ref[wr, :]
        mu = _group_mean(o, avg_ref)
        d = o - mu
        var = _group_mean(d * d, avg_ref)
        o_n = d * lax.rsqrt(var + GN_EPS) * gn_g + gn_b
        gate_r = z_ref[wr, COL_G:COL_G + BRANCH_W]
        y_r = gate_r * jax.nn.sigmoid(gate_r) * o_n
        _add_gated_branch(merged_ref, wr, y_r, hb, 2, w_gate_ref, b_gate, w_branch_ref)

        u = jax.nn.gelu(z_ref[wr, COL_SU:COL_SU + BRANCH_W])
        v = jax.nn.gelu(z_ref[wr, COL_SV:COL_SV + BRANCH_W])
        v = _layer_norm_rows(v, sln_g, sln_b, LN_EPS)
        pieces = []
        for n in range(rn // SGU_LEN):
            v_stack = _stack_masked(v[n * SGU_LEN:(n + 1) * SGU_LEN, :]).astype(BF16)
            pieces.append(_dot(w_s, v_stack) + sgu_bias_ref[...])
        y_s = u * jnp.concatenate(pieces, axis=0)
        _add_gated_branch(merged_ref, wr, y_s, hb, 3, w_gate_ref, b_gate, w_branch_ref)
        merged = merged_ref[wr, :]

        out = _dot(merged.astype(BF16), w_out_ref[...])
        o_ref[rows, :] = _layer_norm_rows(ALPHA * h + out, ln_g, ln_b, LN_EPS)
        lo += rn

    cext_ref[0:CONV_HIST, :] = cext_ref[ts:ts + CONV_HIST, :]
    pext_ref[0:POOL_HIST, :] = pext_ref[ts:ts + POOL_HIST, :]
    state_ref[...] = state


def _retention_constants():
    cc = RET_CC
    heads = np.arange(RET_HEADS, dtype=np.float64)
    log_gamma = np.log1p(-(2.0 ** (-5.0 - heads)))
    idx = np.arange(cc, dtype=np.float64)
    diff = idx[:, None] - idx[None, :]
    decay = np.where(diff >= 0,
                     np.exp(log_gamma[:, None, None] * np.maximum(diff, 0.0)), 0.0)
    dcat = np.transpose(decay, (1, 0, 2)).reshape(cc, RET_HEADS * cc)
    zeta = np.exp(log_gamma[:, None] * (cc - 1 - idx)[None, :])
    xi = np.exp(log_gamma[:, None] * (idx + 1.0)[None, :])
    chunk_decay = np.exp(log_gamma * cc)
    zeta_l = np.repeat(zeta.T, RET_DK, axis=1)
    xi_l = np.repeat(xi.T, RET_DK, axis=1)
    head_of = np.arange(BRANCH_W) // RET_DK
    same = head_of[:, None] == head_of[None, :]
    bm = same.astype(np.float64)
    cd = np.where(same, chunk_decay[head_of][:, None], 0.0)
    f32 = lambda a: jnp.asarray(a, F32)
    return (f32(dcat), f32(xi_l), f32(zeta_l.T), f32(cd), f32(bm),
            jnp.asarray(np.concatenate([bm, bm], axis=0) / RET_DK, BF16))


def _rope_tables(seq):
    half = RET_DK // 2
    inv = ROPE_BASE ** (-jnp.arange(half, dtype=F32) / half)
    ang = jnp.arange(seq).astype(F32)[:, None] * inv[None, :]
    cos = jnp.cos(ang)
    sin = jnp.sin(ang)
    reps = ROPE_LANES // RET_DK
    cos_l = jnp.tile(jnp.concatenate([cos, cos], axis=1), (1, reps))
    sin_l = jnp.tile(jnp.concatenate([-sin, sin], axis=1), (1, reps))
    return cos_l, sin_l


def _pool_inverse_counts():
    assert POOL_HIST >= max(POOL_WINDOWS)
    win = np.repeat(np.asarray(POOL_WINDOWS, np.float64), POOL_GC)
    head = np.minimum(np.arange(POOL_HIST)[:, None] + 1.0, win[None, :])
    steady = np.broadcast_to(win[None, :], (POOL_HIST, BRANCH_W))
    return jnp.asarray(1.0 / np.concatenate([head, steady], axis=0), F32)


def _mixer_layer(x3d, wts, p, consts, layer, cast_stacks, cast_scales, cast_layer):
    bsz, seq, _ = x3d.shape
    ts = MIX_TS
    nseq = seq // ts
    single = pl.Buffered(1)

    def lspec(shape):
        nd = len(shape)
        return pl.BlockSpec((None,) + shape, lambda b, s: (layer,) + (0,) * nd,
                            pipeline_mode=single)

    def cspec(shape):
        nd = len(shape)
        return pl.BlockSpec(shape, lambda b, s: (0,) * nd, pipeline_mode=single)

    casts = [_cast_io(st, cast_layer, bsz * nseq, lambda b, s: b * nseq + s) for st in cast_stacks]
    tile = pl.BlockSpec((None, ts, D_MODEL), lambda b, s: (b, s, 0))
    table = pl.BlockSpec((ts, ROPE_LANES), lambda b, s: (s, 0))
    cos_l, sin_l, icnt, dcat, xi_l, zeta_l, cd, bm, avg = consts
    w_in, w_gate, w_branch, w_out = wts
    in_specs = [
        tile,
        cspec(w_in.shape), cspec(w_gate.shape), cspec(p["b_gate"].shape),
        cspec(w_branch.shape), cspec(w_out.shape),
        lspec((3, BRANCH_W)), lspec((BRANCH_W, BRANCH_W)), cspec(p["pool_scale"].shape),
        cspec(p["gn_g"].shape), cspec(p["gn_b"].shape),
        cspec(p["sln_g"].shape), cspec(p["sln_b"].shape),
        lspec((SGU_LEN, SGU_GROUPS * SGU_LEN)), lspec((SGU_LEN, BRANCH_W)),
        table, table, cspec(icnt.shape),
        cspec(dcat.shape), cspec(xi_l.shape), cspec(zeta_l.shape),
        cspec(cd.shape), cspec(bm.shape), cspec(avg.shape),
        cspec(p["ln_g"].shape), cspec(p["ln_b"].shape),
    ] + [c[0] for c in casts]
    assert len(set(MIX_SUBS)) == 1, "alternating work scratch needs equal sub-tiles"
    work = 2 * MIX_SUBS[0]
    scratch = [
        pltpu.VMEM((work, IN_COLS), F32),
        pltpu.VMEM((CONV_HIST + ts, BRANCH_W), F32),
        pltpu.VMEM((POOL_HIST + ts, BRANCH_W), F32),
        pltpu.VMEM((work, BRANCH_W), F32),
        pltpu.VMEM((work, BRANCH_W), F32),
        pltpu.VMEM((work, BRANCH_W), F32),
        pltpu.VMEM((BRANCH_W, BRANCH_W), F32),
        pltpu.VMEM((work, D_MODEL), F32),
    ]
    outs = pl.pallas_call(
        functools.partial(_mixer_kernel, tuple(cast_scales), layer),
        grid=(bsz, nseq),
        in_specs=in_specs,
        out_specs=[tile] + [c[1] for c in casts],
        out_shape=[jax.ShapeDtypeStruct((bsz, seq, D_MODEL), F32)] + [c[2] for c in casts],
        scratch_shapes=scratch,
        compiler_params=pltpu.CompilerParams(
            dimension_semantics=("arbitrary", "arbitrary"), vmem_limit_bytes=VMEM_LIMIT),
        name="mixer_ln",
    )(x3d, w_in, w_gate, p["b_gate"], w_branch, w_out,
      p["conv_w"], p["pool_bd"], p["pool_scale"], p["gn_g"], p["gn_b"],
      p["sln_g"], p["sln_b"], p["sgu_w"], p["sgu_bias"],
      cos_l, sin_l, icnt, dcat, xi_l, zeta_l, cd, bm, avg, p["ln_g"], p["ln_b"], *cast_stacks)
    return outs[0], outs[1:]


def _prep_mixer_params(conv_w, pool_w, pool_scale, ret_gn_g, ret_gn_b, sgu_ln_g, sgu_ln_b,
                       sgu_w, sgu_b, b_gate, ln2_g, ln2_b):
    depth = conv_w.shape[0]
    groups = pool_w.shape[1]
    eye = jnp.eye(groups, dtype=pool_w.dtype)
    pool_bd = jnp.einsum("lgcd,gh->lgchd", pool_w, eye).reshape(depth, BRANCH_W, BRANCH_W)
    return {
        "b_gate": b_gate,
        "conv_w": conv_w, "pool_bd": pool_bd.astype(BF16), "pool_scale": pool_scale,
        "gn_g": ret_gn_g, "gn_b": ret_gn_b,
        "sln_g": sgu_ln_g, "sln_b": sgu_ln_b,
        "sgu_w": jnp.transpose(sgu_w, (0, 2, 1, 3)).reshape(depth, SGU_LEN, SGU_GROUPS * SGU_LEN),
        "sgu_bias": jnp.repeat(jnp.transpose(sgu_b, (0, 2, 1)), BRANCH_W // SGU_GROUPS, axis=2),
        "ln_g": ln2_g, "ln_b": ln2_b,
    }


def kernel(x, ffn1_w1, ffn1_w2, ln1_g, ln1_b, w_in, conv_w, pool_w, pool_scale, ret_gn_g, ret_gn_b,
           sgu_ln_g, sgu_ln_b, sgu_w, sgu_b, w_branch, w_gate, b_gate, w_out, ln2_g, ln2_b,
           ffn2_w1, ffn2_w2, ln3_g, ln3_b):
    bsz, seq, d = x.shape
    depth = ffn1_w1.shape[0]
    mp = _prep_mixer_params(conv_w, pool_w, pool_scale, ret_gn_g, ret_gn_b, sgu_ln_g, sgu_ln_b,
                            sgu_w, sgu_b, b_gate, ln2_g, ln2_b)
    consts = _rope_tables(seq) + (_pool_inverse_counts(),) + _retention_constants()
    mixer_stacks = (w_in, w_gate, w_branch.reshape(depth, N_BRANCH * BRANCH_W, d), w_out)
    ffn1_stacks = (ffn1_w1, ffn1_w2)
    ffn2_stacks = (ffn2_w1, ffn2_w2)
    ffn_scales = (FFN_W1_SCALE, FFN_W2_SCALE)
    f1 = (_prescale(ffn1_w1[0], FFN_W1_SCALE).astype(BF16),
          _prescale(ffn1_w2[0], FFN_W2_SCALE).astype(BF16))
    x2 = x.reshape(bsz * seq, d)
    for l in range(depth):
        x2, mw = _ffn_layer(x2, f1[0], f1[1], ln1_g, ln1_b, l, mixer_stacks, MIXER_SCALES, l)
        x3, f2 = _mixer_layer(x2.reshape(bsz, seq, d), mw, mp, consts, l,
                              ffn2_stacks, ffn_scales, l)
        last = l + 1 == depth
        x2, f1 = _ffn_layer(x3.reshape(bsz * seq, d), f2[0], f2[1], ln3_g, ln3_b, l,
                            () if last else ffn1_stacks, () if last else ffn_scales, l + 1)
    return x2.reshape(bsz, seq, d)
```

```python
import functools
import math

import jax
import jax.numpy as jnp
import numpy as np
from jax import lax
from jax.experimental import pallas as pl
from jax.experimental.pallas import tpu as pltpu

F32 = jnp.float32
BF16 = jnp.bfloat16

D_MODEL = 1024
DEPTH = 4
CHUNK = 64
BRANCH_W = 256
N_BRANCH = 4
POOL_WINDOWS = (2, 4, 8, 16)
POOL_GC = BRANCH_W // len(POOL_WINDOWS)
RET_HEADS = 4
RET_DK = BRANCH_W // RET_HEADS
ROPE_BASE = 10000.0
SGU_LEN = 128
SGU_GROUPS = 4
D_FF = 2816
ALPHA = (2.0 * DEPTH) ** 0.25
LN_EPS = 1e-5
GN_EPS = 1e-5
IN_COLS = 10 * BRANCH_W

COL_BG, COL_CG, COL_XIN, COL_POOL = 0, 256, 512, 768
COL_Q, COL_K, COL_V, COL_G = 1024, 1280, 1536, 1792
COL_SU, COL_SV = 2048, 2304

FFN_SUBS = (256, 256, 256, 256)
FFN_TM = sum(FFN_SUBS)
MIX_SUBS = (512,)
MIX_TS = sum(MIX_SUBS)
RET_CC = 128
CONV_HIST = 8
ROPE_LANES = 128
POOL_HIST = 16
MERGE_COLS = 512
FFN_F_CHUNKS = ((0, 1536), (1536, 2816))
CAST_BLOCKS = 16
VMEM_LIMIT = 56 * 1024 * 1024


def _dot(a, b):
    return jnp.dot(a, b, preferred_element_type=F32)


def _layer_norm_rows(y, g, b, eps):
    mu = jnp.mean(y, axis=-1, keepdims=True)
    d = y - mu
    var = jnp.mean(d * d, axis=-1, keepdims=True)
    return d * lax.rsqrt(var + eps) * g + b


def _cast_io(stack, layer, steps, step_of):
    _, rows, cols = stack.shape
    blocks = math.gcd(steps, CAST_BLOCKS)
    slab = rows // blocks
    assert slab * blocks == rows and slab % 16 == 0
    per = steps // blocks
    in_spec = pl.BlockSpec((None, slab, cols), lambda *g: (layer, step_of(*g) // per, 0))
    out_spec = pl.BlockSpec((slab, cols), lambda *g: (step_of(*g) // per, 0))
    return in_spec, out_spec, jax.ShapeDtypeStruct((rows, cols), BF16)


def _prescale(w, scale):
    if isinstance(scale, tuple):
        split, left, right = scale
        col = lax.broadcasted_iota(jnp.int32, (1, w.shape[-1]), 1)
        return w * jnp.where(col < split, left, right).astype(w.dtype)
    return w if scale == 1.0 else w * scale


def _convert_slabs(src_refs, dst_refs, scales):
    for s_ref, d_ref, scale in zip(src_refs, dst_refs, scales, strict=True):
        d_ref[...] = _prescale(s_ref[...], scale).astype(BF16)


FFN_W1_SCALE = (D_FF, 0.5, 1.0)
FFN_W2_SCALE = 0.5
MIXER_SCALES = (1.0, 0.5, 0.5, 1.0)


def _ffn_kernel(cast_scales, layer, x_ref, w1_ref, w2_ref, g_ref, b_ref, *refs):
    n_cast = len(cast_scales)
    cast_src, o_ref, cast_dst = refs[:n_cast], refs[n_cast], refs[n_cast + 1:]
    _convert_slabs(cast_src, cast_dst, cast_scales)
    ln_g = g_ref[layer:layer + 1, :]
    ln_b = b_ref[layer:layer + 1, :]
    lo = 0
    for rows_n in FFN_SUBS:
        rows = slice(lo, lo + rows_n)
        lo += rows_n
        x = x_ref[rows, :]
        xb = x.astype(BF16)
        ffn = None
        for c_lo, c_hi in FFN_F_CHUNKS:
            half_gate = _dot(xb, w1_ref[:, c_lo:c_hi])
            up = _dot(xb, w1_ref[:, D_FF + c_lo:D_FF + c_hi])
            act = (half_gate * (1.0 + jnp.tanh(half_gate)) * up).astype(BF16)
            part = _dot(act, w2_ref[c_lo:c_hi, :])
            ffn = part if ffn is None else ffn + part
        y = ALPHA * x + ffn
        o_ref[rows, :] = _layer_norm_rows(y, ln_g, ln_b, LN_EPS)


def _ffn_layer(x2d, w1, w2, g_all, b_all, layer, cast_stacks, cast_scales, cast_layer):
    m = x2d.shape[0]
    steps = m // FFN_TM
    whole = lambda i: (0, 0)
    single = pl.Buffered(1)
    casts = [_cast_io(s, cast_layer, steps, lambda i: i) for s in cast_stacks]
    outs = pl.pallas_call(
        functools.partial(_ffn_kernel, tuple(cast_scales), layer),
        grid=(steps,),
        in_specs=[
            pl.BlockSpec((FFN_TM, D_MODEL), lambda i: (i, 0)),
            pl.BlockSpec((D_MODEL, 2 * D_FF), whole, pipeline_mode=single),
            pl.BlockSpec((D_FF, D_MODEL), whole, pipeline_mode=single),
            pl.BlockSpec(g_all.shape, whole, pipeline_mode=single),
            pl.BlockSpec(b_all.shape, whole, pipeline_mode=single),
        ] + [c[0] for c in casts],
        out_specs=[pl.BlockSpec((FFN_TM, D_MODEL), lambda i: (i, 0))] + [c[1] for c in casts],
        out_shape=[jax.ShapeDtypeStruct((m, D_MODEL), F32)] + [c[2] for c in casts],
        compiler_params=pltpu.CompilerParams(
            dimension_semantics=("arbitrary",), vmem_limit_bytes=VMEM_LIMIT),
        name="ffn_ln",
    )(x2d, w1, w2, g_all, b_all, *cast_stacks)
    return outs[0], outs[1:]


def _lane_group_mask(shape, group):
    lane = lax.broadcasted_iota(jnp.int32, shape, 1)
    return (lane // RET_DK) == group


def _stack_masked(x):
    return jnp.concatenate(
        [jnp.where(_lane_group_mask(x.shape, g), x, 0.0) for g in range(RET_HEADS)], axis=0)


def _head_columns(x_t):
    rows, cc = x_t.shape
    blocks = []
    for g in range(RET_HEADS):
        lo, hi = g * RET_DK, (g + 1) * RET_DK
        parts = [jnp.zeros((lo, cc), x_t.dtype)] if lo else []
        parts.append(x_t[lo:hi, :])
        if hi < rows:
            parts.append(jnp.zeros((rows - hi, cc), x_t.dtype))
        blocks.append(jnp.concatenate(parts, axis=0))
    return jnp.concatenate(blocks, axis=1)


def _group_mean(x, avg_ref):
    hi = x.astype(BF16)
    lo = (x - hi.astype(F32)).astype(BF16)
    return _dot(jnp.concatenate([hi, lo], axis=1), avg_ref[...])


def _add_gated_branch(acc_ref, rows, y, hb, n, w_gate_ref, half_b_gate, w_branch_ref):
    yb = y.astype(BF16)
    for lo in range(0, D_MODEL, MERGE_COLS):
        cols = slice(lo, lo + MERGE_COLS)
        gcols = slice(n * D_MODEL + lo, n * D_MODEL + lo + MERGE_COLS)
        gate2 = 1.0 + jnp.tanh(_dot(hb, w_gate_ref[:, gcols]) + half_b_gate[:, gcols])
        term = gate2 * _dot(yb, w_branch_ref[n * BRANCH_W:(n + 1) * BRANCH_W, cols])
        if n == 0:
            acc_ref[rows, cols] = term
        else:
            acc_ref[rows, cols] += term


def _mixer_kernel(cast_scales, layer,
                  h_ref, w_in_ref, w_gate_ref, b_gate_ref, w_branch_ref, w_out_ref,
                  conv_w_ref, pool_bd_ref, pool_scale_ref, gn_g_ref, gn_b_ref,
                  sln_g_ref, sln_b_ref, sgu_w_ref, sgu_bias_ref,
                  cos_ref, sin_ref, icnt_ref, dcat_ref, xi_ref, zeta_ref, cd_ref, bm_ref, avg_ref,
                  ln_g_ref, ln_b_ref, *refs):
    n_cast = len(cast_scales)
    cast_src, o_ref, cast_dst = refs[:n_cast], refs[n_cast], refs[n_cast + 1:2 * n_cast + 1]
    (z_ref, cext_ref, pext_ref, q_ref, k_ref, ret_ref, state_ref,
     merged_ref) = refs[2 * n_cast + 1:]
    ts = h_ref.shape[0]
    s = pl.program_id(1)

    _convert_slabs(cast_src, cast_dst, cast_scales)

    @pl.when(s == 0)
    def _():
        cext_ref[0:CONV_HIST, :] = jnp.zeros((CONV_HIST, BRANCH_W), F32)
        pext_ref[0:POOL_HIST, :] = jnp.zeros((POOL_HIST, BRANCH_W), F32)
        state_ref[...] = jnp.zeros_like(state_ref)

    wi = lax.broadcasted_iota(jnp.int32, (SGU_LEN, SGU_GROUPS * SGU_LEN), 0)
    wj = lax.broadcasted_iota(jnp.int32, (SGU_LEN, SGU_GROUPS * SGU_LEN), 1) % SGU_LEN
    w_s = jnp.where((wj // CHUNK) <= (wi // CHUNK), sgu_w_ref[...], 0.0).astype(BF16)
    cw = conv_w_ref[...]
    cd = cd_ref[...]
    bm = bm_ref[...]
    state = state_ref[...]
    layer_row = lambda ref: ref[layer:layer + 1, :]
    b_gate, pool_scale = 0.5 * layer_row(b_gate_ref), layer_row(pool_scale_ref)
    gn_g, gn_b = layer_row(gn_g_ref), layer_row(gn_b_ref)
    sln_g, sln_b = layer_row(sln_g_ref), layer_row(sln_b_ref)
    ln_g, ln_b = layer_row(ln_g_ref), layer_row(ln_b_ref)

    lo = 0
    for j, rn in enumerate(MIX_SUBS):
        rows = slice(lo, lo + rn)
        wr = slice((j % 2) * rn, (j % 2 + 1) * rn)
        h = h_ref[rows, :]
        hb = h.astype(BF16)
        z_ref[wr, :] = _dot(hb, w_in_ref[...])

        c = z_ref[wr, COL_CG:COL_CG + BRANCH_W] * z_ref[wr, COL_XIN:COL_XIN + BRANCH_W]
        at = CONV_HIST + lo
        cext_ref[at:at + rn, :] = c
        conv = (cw[0:1, :] * cext_ref[at - 2:at - 2 + rn, :]
                + cw[1:2, :] * cext_ref[at - 1:at - 1 + rn, :]
                + cw[2:3, :] * c)
        y_a = z_ref[wr, COL_BG:COL_BG + BRANCH_W] * conv
        _add_gated_branch(merged_ref, wr, y_a, hb, 0, w_gate_ref, b_gate, w_branch_ref)

        zp = z_ref[wr, COL_POOL:COL_POOL + BRANCH_W]
        pext_ref[POOL_HIST + lo:POOL_HIST + lo + rn, :] = zp
        half = BRANCH_W // 2
        low_group = lax.broadcasted_iota(jnp.int32, (rn, half), 1) < POOL_GC
        ext = pext_ref[lo:lo + POOL_HIST + rn, 0:half]
        s2 = ext + pltpu.roll(ext, 1, 0)
        s4 = s2 + pltpu.roll(s2, 2, 0)
        wsum_lo = jnp.where(low_group, s2[POOL_HIST:], s4[POOL_HIST:])
        ext = pext_ref[lo:lo + POOL_HIST + rn, half:BRANCH_W]
        s2 = ext + pltpu.roll(ext, 1, 0)
        s4 = s2 + pltpu.roll(s2, 2, 0)
        s8 = s4 + pltpu.roll(s4, 4, 0)
        s16 = s8 + pltpu.roll(s8, 8, 0)
        wsum_hi = jnp.where(low_group, s8[POOL_HIST:], s16[POOL_HIST:])
        wsum = jnp.concatenate([wsum_lo, wsum_hi], axis=1)
        lane = lax.broadcasted_iota(jnp.int32, (rn, BRANCH_W), 1)
        mixed = wsum * icnt_ref[POOL_HIST:POOL_HIST + 1, :] - zp
        if j == 0:
            inv_cnt = jnp.where(s == 0, icnt_ref[0:POOL_HIST, :], icnt_ref[POOL_HIST:, :])
            mixed = jnp.concatenate(
                [wsum[0:POOL_HIST, :] * inv_cnt - zp[0:POOL_HIST, :], mixed[POOL_HIST:, :]], axis=0)
        y_p = _dot(mixed.astype(BF16), pool_bd_ref[...]) * pool_scale
        _add_gated_branch(merged_ref, wr, y_p, hb, 1, w_gate_ref, b_gate, w_branch_ref)

        cos = jnp.concatenate([cos_ref[rows, :]] * (BRANCH_W // ROPE_LANES), axis=1)
        sin = jnp.concatenate([sin_ref[rows, :]] * (BRANCH_W // ROPE_LANES), axis=1)
        first_half = (lane % RET_DK) < (RET_DK // 2)

        def rope(x):
            swapped = jnp.where(first_half, pltpu.roll(x, BRANCH_W - RET_DK // 2, 1),
                                pltpu.roll(x, RET_DK // 2, 1))
            return x * cos + swapped * sin

        q_ref[wr, :] = rope(z_ref[wr, COL_Q:COL_Q + BRANCH_W])
        k_ref[wr, :] = rope(z_ref[wr, COL_K:COL_K + BRANCH_W]) * (RET_DK ** -0.5)

        for n in range(rn // RET_CC):
            rr = slice(wr.start + n * RET_CC, wr.start + (n + 1) * RET_CC)
            qc = q_ref[rr, :]
            kc = k_ref[rr, :]
            vc = z_ref[rr, COL_V:COL_V + BRANCH_W]
            k_t = jnp.transpose(kc)
            v_stack = _stack_masked(vc).astype(BF16)
            scores = _dot(qc.astype(BF16), _head_columns(k_t).astype(BF16))
            probs = (scores * dcat_ref[...]).astype(BF16)
            inner = _dot(probs, v_stack)
            cross = _dot((qc * xi_ref[...]).astype(BF16), state.astype(BF16))
            ret_ref[rr, :] = inner + cross
            kv = _dot((k_t * zeta_ref[...]).astype(BF16), vc.astype(BF16))
            state = state * cd + kv * bm

        o = ret_ref[wr, :]
        mu = _group_mean(o, avg_ref)
        d = o - mu
        var = _group_mean(d * d, avg_ref)
        o_n = d * lax.rsqrt(var + GN_EPS) * gn_g + gn_b
        gate_r = z_ref[wr, COL_G:COL_G + BRANCH_W]
        y_r = gate_r * jax.nn.sigmoid(gate_r) * o_n
        _add_gated_branch(merged_ref, wr, y_r, hb, 2, w_gate_ref, b_gate, w_branch_ref)

        u = jax.nn.gelu(z_ref[wr, COL_SU:COL_SU + BRANCH_W])
        v = jax.nn.gelu(z_ref[wr, COL_SV:COL_SV + BRANCH_W])
        v = _layer_norm_rows(v, sln_g, sln_b, LN_EPS)
        pieces = []
        for n in range(rn // SGU_LEN):
            v_stack = _stack_masked(v[n * SGU_LEN:(n + 1) * SGU_LEN, :]).astype(BF16)
            pieces.append(_dot(w_s, v_stack) + sgu_bias_ref[...])
        y_s = u * jnp.concatenate(pieces, axis=0)
        _add_gated_branch(merged_ref, wr, y_s, hb, 3, w_gate_ref, b_gate, w_branch_ref)
        merged = merged_ref[wr, :]

        out = _dot(merged.astype(BF16), w_out_ref[...])
        o_ref[rows, :] = _layer_norm_rows(ALPHA * h + out, ln_g, ln_b, LN_EPS)
        lo += rn

    cext_ref[0:CONV_HIST, :] = cext_ref[ts:ts + CONV_HIST, :]
    pext_ref[0:POOL_HIST, :] = pext_ref[ts:ts + POOL_HIST, :]
    state_ref[...] = state


def _retention_constants():
    cc = RET_CC
    heads = np.arange(RET_HEADS, dtype=np.float64)
    log_gamma = np.log1p(-(2.0 ** (-5.0 - heads)))
    idx = np.arange(cc, dtype=np.float64)
    diff = idx[:, None] - idx[None, :]
    decay = np.where(diff >= 0,
                     np.exp(log_gamma[:, None, None] * np.maximum(diff, 0.0)), 0.0)
    dcat = np.transpose(decay, (1, 0, 2)).reshape(cc, RET_HEADS * cc)
    zeta = np.exp(log_gamma[:, None] * (cc - 1 - idx)[None, :])
    xi = np.exp(log_gamma[:, None] * (idx + 1.0)[None, :])
    chunk_decay = np.exp(log_gamma * cc)
    zeta_l = np.repeat(zeta.T, RET_DK, axis=1)
    xi_l = np.repeat(xi.T, RET_DK, axis=1)
    head_of = np.arange(BRANCH_W) // RET_DK
    same = head_of[:, None] == head_of[None, :]
    bm = same.astype(np.float64)
    cd = np.where(same, chunk_decay[head_of][:, None], 0.0)
    f32 = lambda a: jnp.asarray(a, F32)
    return (f32(dcat), f32(xi_l), f32(zeta_l.T), f32(cd), f32(bm),
            jnp.asarray(np.concatenate([bm, bm], axis=0) / RET_DK, BF16))


def _rope_tables(seq):
    half = RET_DK // 2
    inv = ROPE_BASE ** (-jnp.arange(half, dtype=F32) / half)
    ang = jnp.arange(seq).astype(F32)[:, None] * inv[None, :]
    cos = jnp.cos(ang)
    sin = jnp.sin(ang)
    reps = ROPE_LANES // RET_DK
    cos_l = jnp.tile(jnp.concatenate([cos, cos], axis=1), (1, reps))
    sin_l = jnp.tile(jnp.concatenate([-sin, sin], axis=1), (1, reps))
    return cos_l, sin_l


def _pool_inverse_counts():
    assert POOL_HIST >= max(POOL_WINDOWS)
    win = np.repeat(np.asarray(POOL_WINDOWS, np.float64), POOL_GC)
    head = np.minimum(np.arange(POOL_HIST)[:, None] + 1.0, win[None, :])
    steady = np.broadcast_to(win[None, :], (POOL_HIST, BRANCH_W))
    return jnp.asarray(1.0 / np.concatenate([head, steady], axis=0), F32)


def _mixer_layer(x3d, wts, p, consts, layer, cast_stacks, cast_scales, cast_layer):
    bsz, seq, _ = x3d.shape
    ts = MIX_TS
    nseq = seq // ts
    single = pl.Buffered(1)

    def lspec(shape):
        nd = len(shape)
        return pl.BlockSpec((None,) + shape, lambda b, s: (layer,) + (0,) * nd,
                            pipeline_mode=single)

    def cspec(shape):
        nd = len(shape)
        return pl.BlockSpec(shape, lambda b, s: (0,) * nd, pipeline_mode=single)

    casts = [_cast_io(st, cast_layer, bsz * nseq, lambda b, s: b * nseq + s) for st in cast_stacks]
    tile = pl.BlockSpec((None, ts, D_MODEL), lambda b, s: (b, s, 0))
    table = pl.BlockSpec((ts, ROPE_LANES), lambda b, s: (s, 0))
    cos_l, sin_l, icnt, dcat, xi_l, zeta_l, cd, bm, avg = consts
    w_in, w_gate, w_branch, w_out = wts
    in_specs = [
        tile,
        cspec(w_in.shape), cspec(w_gate.shape), cspec(p["b_gate"].shape),
        cspec(w_branch.shape), cspec(w_out.shape),
        lspec((3, BRANCH_W)), lspec((BRANCH_W, BRANCH_W)), cspec(p["pool_scale"].shape),
        cspec(p["gn_g"].shape), cspec(p["gn_b"].shape),
        cspec(p["sln_g"].shape), cspec(p["sln_b"].shape),
        lspec((SGU_LEN, SGU_GROUPS * SGU_LEN)), lspec((SGU_LEN, BRANCH_W)),
        table, table, cspec(icnt.shape),
        cspec(dcat.shape), cspec(xi_l.shape), cspec(zeta_l.shape),
        cspec(cd.shape), cspec(bm.shape), cspec(avg.shape),
        cspec(p["ln_g"].shape), cspec(p["ln_b"].shape),
    ] + [c[0] for c in casts]
    assert len(set(MIX_SUBS)) == 1, "alternating work scratch needs equal sub-tiles"
    work = 2 * MIX_SUBS[0]
    scratch = [
        pltpu.VMEM((work, IN_COLS), F32),
        pltpu.VMEM((CONV_HIST + ts, BRANCH_W), F32),
        pltpu.VMEM((POOL_HIST + ts, BRANCH_W), F32),
        pltpu.VMEM((work, BRANCH_W), F32),
        pltpu.VMEM((work, BRANCH_W), F32),
        pltpu.VMEM((work, BRANCH_W), F32),
        pltpu.VMEM((BRANCH_W, BRANCH_W), F32),
        pltpu.VMEM((work, D_MODEL), F32),
    ]
    outs = pl.pallas_call(
        functools.partial(_mixer_kernel, tuple(cast_scales), layer),
        grid=(bsz, nseq),
        in_specs=in_specs,
        out_specs=[tile] + [c[1] for c in casts],
        out_shape=[jax.ShapeDtypeStruct((bsz, seq, D_MODEL), F32)] + [c[2] for c in casts],
        scratch_shapes=scratch,
        compiler_params=pltpu.CompilerParams(
            dimension_semantics=("arbitrary", "arbitrary"), vmem_limit_bytes=VMEM_LIMIT),
        name="mixer_ln",
    )(x3d, w_in, w_gate, p["b_gate"], w_branch, w_out,
      p["conv_w"], p["pool_bd"], p["pool_scale"], p["gn_g"], p["gn_b"],
      p["sln_g"], p["sln_b"], p["sgu_w"], p["sgu_bias"],
      cos_l, sin_l, icnt, dcat, xi_l, zeta_l, cd, bm, avg, p["ln_g"], p["ln_b"], *cast_stacks)
    return outs[0], outs[1:]


def _prep_mixer_params(conv_w, pool_w, pool_scale, ret_gn_g, ret_gn_b, sgu_ln_g, sgu_ln_b,
                       sgu_w, sgu_b, b_gate, ln2_g, ln2_b):
    depth = conv_w.shape[0]
    groups = pool_w.shape[1]
    eye = jnp.eye(groups, dtype=pool_w.dtype)
    pool_bd = jnp.einsum("lgcd,gh->lgchd", pool_w, eye).reshape(depth, BRANCH_W, BRANCH_W)
    return {
        "b_gate": b_gate,
        "conv_w": conv_w, "pool_bd": pool_bd.astype(BF16), "pool_scale": pool_scale,
        "gn_g": ret_gn_g, "gn_b": ret_gn_b,
        "sln_g": sgu_ln_g, "sln_b": sgu_ln_b,
        "sgu_w": jnp.transpose(sgu_w, (0, 2, 1, 3)).reshape(depth, SGU_LEN, SGU_GROUPS * SGU_LEN),
        "sgu_bias": jnp.repeat(jnp.transpose(sgu_b, (0, 2, 1)), BRANCH_W // SGU_GROUPS, axis=2),
        "ln_g": ln2_g, "ln_b": ln2_b,
    }


def kernel(x, ffn1_w1, ffn1_w2, ln1_g, ln1_b, w_in, conv_w, pool_w, pool_scale, ret_gn_g, ret_gn_b,
           sgu_ln_g, sgu_ln_b, sgu_w, sgu_b, w_branch, w_gate, b_gate, w_out, ln2_g, ln2_b,
           ffn2_w1, ffn2_w2, ln3_g, ln3_b):
    bsz, seq, d = x.shape
    depth = ffn1_w1.shape[0]
    mp = _prep_mixer_params(conv_w, pool_w, pool_scale, ret_gn_g, ret_gn_b, sgu_ln_g, sgu_ln_b,
                            sgu_w, sgu_b, b_gate, ln2_g, ln2_b)
    consts = _rope_tables(seq) + (_pool_inverse_counts(),) + _retention_constants()
    mixer_stacks = (w_in, w_gate, w_branch.reshape(depth, N_BRANCH * BRANCH_W, d), w_out)
    ffn1_stacks = (ffn1_w1, ffn1_w2)
    ffn2_stacks = (ffn2_w1, ffn2_w2)
    ffn_scales = (FFN_W1_SCALE, FFN_W2_SCALE)
    f1 = (_prescale(ffn1_w1[0], FFN_W1_SCALE).astype(BF16),
          _prescale(ffn1_w2[0], FFN_W2_SCALE).astype(BF16))
    x2 = x.reshape(bsz * seq, d)
    for l in range(depth):
        x2, mw = _ffn_layer(x2, f1[0], f1[1], ln1_g, ln1_b, l, mixer_stacks, MIXER_SCALES, l)
        x3, f2 = _mixer_layer(x2.reshape(bsz, seq, d), mw, mp, consts, l,
                              ffn2_stacks, ffn_scales, l)
        last = l + 1 == depth
        x2, f1 = _ffn_layer(x3.reshape(bsz * seq, d), f2[0], f2[1], ln3_g, ln3_b, l,
                            () if last else ffn1_stacks, () if last else ffn_scales, l + 1)
    return x2.reshape(bsz, seq, d)
```

```python
import functools
import math

import jax
import jax.numpy as jnp
import numpy as np
from jax import lax
from jax.experimental import pallas as pl
from jax.experimental.pallas import tpu as pltpu

F32 = jnp.float32
BF16 = jnp.bfloat16

D_MODEL = 1024
DEPTH = 4
CHUNK = 64
BRANCH_W = 256
N_BRANCH = 4
POOL_WINDOWS = (2, 4, 8, 16)
POOL_GC = BRANCH_W // len(POOL_WINDOWS)
RET_HEADS = 4
RET_DK = BRANCH_W // RET_HEADS
ROPE_BASE = 10000.0
SGU_LEN = 128
SGU_GROUPS = 4
D_FF = 2816
ALPHA = (2.0 * DEPTH) ** 0.25
LN_EPS = 1e-5
GN_EPS = 1e-5
IN_COLS = 10 * BRANCH_W

COL_BG, COL_CG, COL_XIN, COL_POOL = 0, 256, 512, 768
COL_Q, COL_K, COL_V, COL_G = 1024, 1280, 1536, 1792
COL_SU, COL_SV = 2048, 2304

FFN_SUBS = (256, 256, 256, 256)
FFN_TM = sum(FFN_SUBS)
MIX_SUBS = (512,)
MIX_TS = sum(MIX_SUBS)
RET_CC = 128
CONV_HIST = 8
ROPE_LANES = 128
POOL_HIST = 16
MERGE_COLS = 512
FFN_F_CHUNKS = ((0, 1536), (1536, 2816))
CAST_BLOCKS = 16
VMEM_LIMIT = 56 * 1024 * 1024


def _dot(a, b):
    return jnp.dot(a, b, preferred_element_type=F32)


def _layer_norm_rows(y, g, b, eps):
    mu = jnp.mean(y, axis=-1, keepdims=True)
    d = y - mu
    var = jnp.mean(d * d, axis=-1, keepdims=True)
    return d * lax.rsqrt(var + eps) * g + b


def _cast_io(stack, layer, steps, step_of):
    _, rows, cols = stack.shape
    blocks = math.gcd(steps, CAST_BLOCKS)
    slab = rows // blocks
    assert slab * blocks == rows and slab % 16 == 0
    per = steps // blocks
    in_spec = pl.BlockSpec((None, slab, cols), lambda *g: (layer, step_of(*g) // per, 0))
    out_spec = pl.BlockSpec((slab, cols), lambda *g: (step_of(*g) // per, 0))
    return in_spec, out_spec, jax.ShapeDtypeStruct((rows, cols), BF16)


def _prescale(w, scale):
    if isinstance(scale, tuple):
        split, left, right = scale
        col = lax.broadcasted_iota(jnp.int32, (1, w.shape[-1]), 1)
        return w * jnp.where(col < split, left, right).astype(w.dtype)
    return w if scale == 1.0 else w * scale


def _convert_slabs(src_refs, dst_refs, scales):
    for s_ref, d_ref, scale in zip(src_refs, dst_refs, scales, strict=True):
        d_ref[...] = _prescale(s_ref[...], scale).astype(BF16)


FFN_W1_SCALE = (D_FF, 0.5, 1.0)
FFN_W2_SCALE = 0.5
MIXER_SCALES = (1.0, 0.5, 0.5, 1.0)


def _ffn_kernel(cast_scales, layer, x_ref, w1_ref, w2_ref, g_ref, b_ref, *refs):
    n_cast = len(cast_scales)
    cast_src, o_ref, cast_dst = refs[:n_cast], refs[n_cast], refs[n_cast + 1:]
    _convert_slabs(cast_src, cast_dst, cast_scales)
    ln_g = g_ref[layer:layer + 1, :]
    ln_b = b_ref[layer:layer + 1, :]
    lo = 0
    for rows_n in FFN_SUBS:
        rows = slice(lo, lo + rows_n)
        lo += rows_n
        x = x_ref[rows, :]
        xb = x.astype(BF16)
        ffn = None
        for c_lo, c_hi in FFN_F_CHUNKS:
            half_gate = _dot(xb, w1_ref[:, c_lo:c_hi])
            up = _dot(xb, w1_ref[:, D_FF + c_lo:D_FF + c_hi])
            act = (half_gate * (1.0 + jnp.tanh(half_gate)) * up).astype(BF16)
            part = _dot(act, w2_ref[c_lo:c_hi, :])
            ffn = part if ffn is None else ffn + part
        y = ALPHA * x + ffn
        o_ref[rows, :] = _layer_norm_rows(y, ln_g, ln_b, LN_EPS)


def _ffn_layer(x2d, w1, w2, g_all, b_all, layer, cast_stacks, cast_scales, cast_layer):
    m = x2d.shape[0]
    steps = m // FFN_TM
    whole = lambda i: (0, 0)
    single = pl.Buffered(1)
    casts = [_cast_io(s, cast_layer, steps, lambda i: i) for s in cast_stacks]
    outs = pl.pallas_call(
        functools.partial(_ffn_kernel, tuple(cast_scales), layer),
        grid=(steps,),
        in_specs=[
            pl.BlockSpec((FFN_TM, D_MODEL), lambda i: (i, 0)),
            pl.BlockSpec((D_MODEL, 2 * D_FF), whole, pipeline_mode=single),
            pl.BlockSpec((D_FF, D_MODEL), whole, pipeline_mode=single),
            pl.BlockSpec(g_all.shape, whole, pipeline_mode=single),
            pl.BlockSpec(b_all.shape, whole, pipeline_mode=single),
        ] + [c[0] for c in casts],
        out_specs=[pl.BlockSpec((FFN_TM, D_MODEL), lambda i: (i, 0))] + [c[1] for c in casts],
        out_shape=[jax.ShapeDtypeStruct((m, D_MODEL), F32)] + [c[2] for c in casts],
        compiler_params=pltpu.CompilerParams(
            dimension_semantics=("arbitrary",), vmem_limit_bytes=VMEM_LIMIT),
        name="ffn_ln",
    )(x2d, w1, w2, g_all, b_all, *cast_stacks)
    return outs[0], outs[1:]


def _lane_group_mask(shape, group):
    lane = lax.broadcasted_iota(jnp.int32, shape, 1)
    return (lane // RET_DK) == group


def _stack_masked(x):
    return jnp.concatenate(
        [jnp.where(_lane_group_mask(x.shape, g), x, 0.0) for g in range(RET_HEADS)], axis=0)


def _head_columns(x_t):
    rows, cc = x_t.shape
    blocks = []
    for g in range(RET_HEADS):
        lo, hi = g * RET_DK, (g + 1) * RET_DK
        parts = [jnp.zeros((lo, cc), x_t.dtype)] if lo else []
        parts.append(x_t[lo:hi, :])
        if hi < rows:
            parts.append(jnp.zeros((rows - hi, cc), x_t.dtype))
        blocks.append(jnp.concatenate(parts, axis=0))
    return jnp.concatenate(blocks, axis=1)


def _group_mean(x, avg_ref):
    hi = x.astype(BF16)
    lo = (x - hi.astype(F32)).astype(BF16)
    return _dot(jnp.concatenate([hi, lo], axis=1), avg_ref[...])


def _add_gated_branch(acc_ref, rows, y, hb, n, w_gate_ref, half_b_gate, w_branch_ref):
    yb = y.astype(BF16)
    for lo in range(0, D_MODEL, MERGE_COLS):
        cols = slice(lo, lo + MERGE_COLS)
        gcols = slice(n * D_MODEL + lo, n * D_MODEL + lo + MERGE_COLS)
        gate2 = 1.0 + jnp.tanh(_dot(hb, w_gate_ref[:, gcols]) + half_b_gate[:, gcols])
        term = gate2 * _dot(yb, w_branch_ref[n * BRANCH_W:(n + 1) * BRANCH_W, cols])
        if n == 0:
            acc_ref[rows, cols] = term
        else:
            acc_ref[rows, cols] += term


def _mixer_kernel(cast_scales, layer,
                  h_ref, w_in_ref, w_gate_ref, b_gate_ref, w_branch_ref, w_out_ref,
                  conv_w_ref, pool_bd_ref, pool_scale_ref, gn_g_ref, gn_b_ref,
                  sln_g_ref, sln_b_ref, sgu_w_ref, sgu_bias_ref,
                  cos_ref, sin_ref, icnt_ref, dcat_ref, xi_ref, zeta_ref, cd_ref, bm_ref, avg_ref,
                  ln_g_ref, ln_b_ref, *refs):
    n_cast = len(cast_scales)
    cast_src, o_ref, cast_dst = refs[:n_cast], refs[n_cast], refs[n_cast + 1:2 * n_cast + 1]
    (z_ref, cext_ref, pext_ref, q_ref, k_ref, ret_ref, state_ref,
     merged_ref) = refs[2 * n_cast + 1:]
    ts = h_ref.shape[0]
    s = pl.program_id(1)

    _convert_slabs(cast_src, cast_dst, cast_scales)

    @pl.when(s == 0)
    def _():
        cext_ref[0:CONV_HIST, :] = jnp.zeros((CONV_HIST, BRANCH_W), F32)
        pext_ref[0:POOL_HIST, :] = jnp.zeros((POOL_HIST, BRANCH_W), F32)
        state_ref[...] = jnp.zeros_like(state_ref)

    wi = lax.broadcasted_iota(jnp.int32, (SGU_LEN, SGU_GROUPS * SGU_LEN), 0)
    wj = lax.broadcasted_iota(jnp.int32, (SGU_LEN, SGU_GROUPS * SGU_LEN), 1) % SGU_LEN
    w_s = jnp.where((wj // CHUNK) <= (wi // CHUNK), sgu_w_ref[...], 0.0).astype(BF16)
    cw = conv_w_ref[...]
    cd = cd_ref[...]
    bm = bm_ref[...]
    state = state_ref[...]
    layer_row = lambda ref: ref[layer:layer + 1, :]
    b_gate, pool_scale = 0.5 * layer_row(b_gate_ref), layer_row(pool_scale_ref)
    gn_g, gn_b = layer_row(gn_g_ref), layer_row(gn_b_ref)
    sln_g, sln_b = layer_row(sln_g_ref), layer_row(sln_b_ref)
    ln_g, ln_b = layer_row(ln_g_ref), layer_row(ln_b_ref)

    lo = 0
    for j, rn in enumerate(MIX_SUBS):
        rows = slice(lo, lo + rn)
        wr = slice((j % 2) * rn, (j % 2 + 1) * rn)
        h = h_ref[rows, :]
        hb = h.astype(BF16)
        z_ref[wr, :] = _dot(hb, w_in_ref[...])

        c = z_ref[wr, COL_CG:COL_CG + BRANCH_W] * z_ref[wr, COL_XIN:COL_XIN + BRANCH_W]
        at = CONV_HIST + lo
        cext_ref[at:at + rn, :] = c
        conv = (cw[0:1, :] * cext_ref[at - 2:at - 2 + rn, :]
                + cw[1:2, :] * cext_ref[at - 1:at - 1 + rn, :]
                + cw[2:3, :] * c)
        y_a = z_ref[wr, COL_BG:COL_BG + BRANCH_W] * conv
        _add_gated_branch(merged_ref, wr, y_a, hb, 0, w_gate_ref, b_gate, w_branch_ref)

        zp = z_ref[wr, COL_POOL:COL_POOL + BRANCH_W]
        pext_ref[POOL_HIST + lo:POOL_HIST + lo + rn, :] = zp
        half = BRANCH_W // 2
        low_group = lax.broadcasted_iota(jnp.int32, (rn, half), 1) < POOL_GC
        ext = pext_ref[lo:lo + POOL_HIST + rn, 0:half]
        s2 = ext + pltpu.roll(ext, 1, 0)
        s4 = s2 + pltpu.roll(s2, 2, 0)
        wsum_lo = jnp.where(low_group, s2[POOL_HIST:], s4[POOL_HIST:])
        ext = pext_ref[lo:lo + POOL_HIST + rn, half:BRANCH_W]
        s2 = ext + pltpu.roll(ext, 1, 0)
        s4 = s2 + pltpu.roll(s2, 2, 0)
        s8 = s4 + pltpu.roll(s4, 4, 0)
        s16 = s8 + pltpu.roll(s8, 8, 0)
        wsum_hi = jnp.where(low_group, s8[POOL_HIST:], s16[POOL_HIST:])
        wsum = jnp.concatenate([wsum_lo, wsum_hi], axis=1)
        lane = lax.broadcasted_iota(jnp.int32, (rn, BRANCH_W), 1)
        mixed = wsum * icnt_ref[POOL_HIST:POOL_HIST + 1, :] - zp
        if j == 0:
            inv_cnt = jnp.where(s == 0, icnt_ref[0:POOL_HIST, :], icnt_ref[POOL_HIST:, :])
            mixed = jnp.concatenate(
                [wsum[0:POOL_HIST, :] * inv_cnt - zp[0:POOL_HIST, :], mixed[POOL_HIST:, :]], axis=0)
        y_p = _dot(mixed.astype(BF16), pool_bd_ref[...]) * pool_scale
        _add_gated_branch(merged_ref, wr, y_p, hb, 1, w_gate_ref, b_gate, w_branch_ref)

        cos = jnp.concatenate([cos_ref[rows, :]] * (BRANCH_W // ROPE_LANES), axis=1)
        sin = jnp.concatenate([sin_ref[rows, :]] * (BRANCH_W // ROPE_LANES), axis=1)
        first_half = (lane % RET_DK) < (RET_DK // 2)

        def rope(x):
            swapped = jnp.where(first_half, pltpu.roll(x, BRANCH_W - RET_DK // 2, 1),
                                pltpu.roll(x, RET_DK // 2, 1))
            return x * cos + swapped * sin

        q_ref[wr, :] = rope(z_ref[wr, COL_Q:COL_Q + BRANCH_W])
        k_ref[wr, :] = rope(z_ref[wr, COL_K:COL_K + BRANCH_W]) * (RET_DK ** -0.5)

        for n in range(rn // RET_CC):
            rr = slice(wr.start + n * RET_CC, wr.start + (n + 1) * RET_CC)
            qc = q_ref[rr, :]
            kc = k_ref[rr, :]
            vc = z_ref[rr, COL_V:COL_V + BRANCH_W]
            k_t = jnp.transpose(kc)
            v_stack = _stack_masked(vc).astype(BF16)
            scores = _dot(qc.astype(BF16), _head_columns(k_t).astype(BF16))
            probs = (scores * dcat_ref[...]).astype(BF16)
            inner = _dot(probs, v_stack)
            cross = _dot((qc * xi_ref[...]).astype(BF16), state.astype(BF16))
            ret_ref[rr, :] = inner + cross
            kv = _dot((k_t * zeta_ref[...]).astype(BF16), vc.astype(BF16))
            state = state * cd + kv * bm

        o = ret_ref[wr, :]
        mu = _group_mean(o, avg_ref)
        d = o - mu
        var = _group_mean(d * d, avg_ref)
        o_n = d * lax.rsqrt(var + GN_EPS) * gn_g + gn_b
        gate_r = z_ref[wr, COL_G:COL_G + BRANCH_W]
        y_r = gate_r * jax.nn.sigmoid(gate_r) * o_n
        _add_gated_branch(merged_ref, wr, y_r, hb, 2, w_gate_ref, b_gate, w_branch_ref)

        u = jax.nn.gelu(z_ref[wr, COL_SU:COL_SU + BRANCH_W])
        v = jax.nn.gelu(z_ref[wr, COL_SV:COL_SV + BRANCH_W])
        v = _layer_norm_rows(v, sln_g, sln_b, LN_EPS)
        pieces = []
        for n in range(rn // SGU_LEN):
            v_stack = _stack_masked(v[n * SGU_LEN:(n + 1) * SGU_LEN, :]).astype(BF16)
            pieces.append(_dot(w_s, v_stack) + sgu_bias_ref[...])
        y_s = u * jnp.concatenate(pieces, axis=0)
        _add_gated_branch(merged_ref, wr, y_s, hb, 3, w_gate_ref, b_gate, w_branch_ref)
        merged = merged_ref[wr, :]

        out = _dot(merged.astype(BF16), w_out_ref[...])
        o_ref[rows, :] = _layer_norm_rows(ALPHA * h + out, ln_g, ln_b, LN_EPS)
        lo += rn

    cext_ref[0:CONV_HIST, :] = cext_ref[ts:ts + CONV_HIST, :]
    pext_ref[0:POOL_HIST, :] = pext_ref[ts:ts + POOL_HIST, :]
    state_ref[...] = state


def _retention_constants():
    cc = RET_CC
    heads = np.arange(RET_HEADS, dtype=np.float64)
    log_gamma = np.log1p(-(2.0 ** (-5.0 - heads)))
    idx = np.arange(cc, dtype=np.float64)
    diff = idx[:, None] - idx[None, :]
    decay = np.where(diff >= 0,
                     np.exp(log_gamma[:, None, None] * np.maximum(diff, 0.0)), 0.0)
    dcat = np.transpose(decay, (1, 0, 2)).reshape(cc, RET_HEADS * cc)
    zeta = np.exp(log_gamma[:, None] * (cc - 1 - idx)[None, :])
    xi = np.exp(log_gamma[:, None] * (idx + 1.0)[None, :])
    chunk_decay = np.exp(log_gamma * cc)
    zeta_l = np.repeat(zeta.T, RET_DK, axis=1)
    xi_l = np.repeat(xi.T, RET_DK, axis=1)
    head_of = np.arange(BRANCH_W) // RET_DK
    same = head_of[:, None] == head_of[None, :]
    bm = same.astype(np.float64)
    cd = np.where(same, chunk_decay[head_of][:, None], 0.0)
    f32 = lambda a: jnp.asarray(a, F32)
    return (f32(dcat), f32(xi_l), f32(zeta_l.T), f32(cd), f32(bm),
            jnp.asarray(np.concatenate([bm, bm], axis=0) / RET_DK, BF16))


def _rope_tables(seq):
    half = RET_DK // 2
    inv = ROPE_BASE ** (-jnp.arange(half, dtype=F32) / half)
    ang = jnp.arange(seq).astype(F32)[:, None] * inv[None, :]
    cos = jnp.cos(ang)
    sin = jnp.sin(ang)
    reps = ROPE_LANES // RET_DK
    cos_l = jnp.tile(jnp.concatenate([cos, cos], axis=1), (1, reps))
    sin_l = jnp.tile(jnp.concatenate([-sin, sin], axis=1), (1, reps))
    return cos_l, sin_l


def _pool_inverse_counts():
    assert POOL_HIST >= max(POOL_WINDOWS)
    win = np.repeat(np.asarray(POOL_WINDOWS, np.float64), POOL_GC)
    head = np.minimum(np.arange(POOL_HIST)[:, None] + 1.0, win[None, :])
    steady = np.broadcast_to(win[None, :], (POOL_HIST, BRANCH_W))
    return jnp.asarray(1.0 / np.concatenate([head, steady], axis=0), F32)


def _mixer_layer(x3d, wts, p, consts, layer, cast_stacks, cast_scales, cast_layer):
    bsz, seq, _ = x3d.shape
    ts = MIX_TS
    nseq = seq // ts
    single = pl.Buffered(1)

    def lspec(shape):
        nd = len(shape)
        return pl.BlockSpec((None,) + shape, lambda b, s: (layer,) + (0,) * nd,
                            pipeline_mode=single)

    def cspec(shape):
        nd = len(shape)
        return pl.BlockSpec(shape, lambda b, s: (0,) * nd, pipeline_mode=single)

    casts = [_cast_io(st, cast_layer, bsz * nseq, lambda b, s: b * nseq + s) for st in cast_stacks]
    tile = pl.BlockSpec((None, ts, D_MODEL), lambda b, s: (b, s, 0))
    table = pl.BlockSpec((ts, ROPE_LANES), lambda b, s: (s, 0))
    cos_l, sin_l, icnt, dcat, xi_l, zeta_l, cd, bm, avg = consts
    w_in, w_gate, w_branch, w_out = wts
    in_specs = [
        tile,
        cspec(w_in.shape), cspec(w_gate.shape), cspec(p["b_gate"].shape),
        cspec(w_branch.shape), cspec(w_out.shape),
        lspec((3, BRANCH_W)), lspec((BRANCH_W, BRANCH_W)), cspec(p["pool_scale"].shape),
        cspec(p["gn_g"].shape), cspec(p["gn_b"].shape),
        cspec(p["sln_g"].shape), cspec(p["sln_b"].shape),
        lspec((SGU_LEN, SGU_GROUPS * SGU_LEN)), lspec((SGU_LEN, BRANCH_W)),
        table, table, cspec(icnt.shape),
        cspec(dcat.shape), cspec(xi_l.shape), cspec(zeta_l.shape),
        cspec(cd.shape), cspec(bm.shape), cspec(avg.shape),
        cspec(p["ln_g"].shape), cspec(p["ln_b"].shape),
    ] + [c[0] for c in casts]
    assert len(set(MIX_SUBS)) == 1, "alternating work scratch needs equal sub-tiles"
    work = min(len(MIX_SUBS), 2) * MIX_SUBS[0]
    scratch = [
        pltpu.VMEM((work, IN_COLS), F32),
        pltpu.VMEM((CONV_HIST + ts, BRANCH_W), F32),
        pltpu.VMEM((POOL_HIST + ts, BRANCH_W), F32),
        pltpu.VMEM((work, BRANCH_W), F32),
        pltpu.VMEM((work, BRANCH_W), F32),
        pltpu.VMEM((work, BRANCH_W), F32),
        pltpu.VMEM((BRANCH_W, BRANCH_W), F32),
        pltpu.VMEM((work, D_MODEL), F32),
    ]
    outs = pl.pallas_call(
        functools.partial(_mixer_kernel, tuple(cast_scales), layer),
        grid=(bsz, nseq),
        in_specs=in_specs,
        out_specs=[tile] + [c[1] for c in casts],
        out_shape=[jax.ShapeDtypeStruct((bsz, seq, D_MODEL), F32)] + [c[2] for c in casts],
        scratch_shapes=scratch,
        compiler_params=pltpu.CompilerParams(
            dimension_semantics=("arbitrary", "arbitrary"), vmem_limit_bytes=VMEM_LIMIT),
        name="mixer_ln",
    )(x3d, w_in, w_gate, p["b_gate"], w_branch, w_out,
      p["conv_w"], p["pool_bd"], p["pool_scale"], p["gn_g"], p["gn_b"],
      p["sln_g"], p["sln_b"], p["sgu_w"], p["sgu_bias"],
      cos_l, sin_l, icnt, dcat, xi_l, zeta_l, cd, bm, avg, p["ln_g"], p["ln_b"], *cast_stacks)
    return outs[0], outs[1:]


def _prep_mixer_params(conv_w, pool_w, pool_scale, ret_gn_g, ret_gn_b, sgu_ln_g, sgu_ln_b,
                       sgu_w, sgu_b, b_gate, ln2_g, ln2_b):
    depth = conv_w.shape[0]
    groups = pool_w.shape[1]
    eye = jnp.eye(groups, dtype=pool_w.dtype)
    pool_bd = jnp.einsum("lgcd,gh->lgchd", pool_w, eye).reshape(depth, BRANCH_W, BRANCH_W)
    return {
        "b_gate": b_gate,
        "conv_w": conv_w, "pool_bd": pool_bd.astype(BF16), "pool_scale": pool_scale,
        "gn_g": ret_gn_g, "gn_b": ret_gn_b,
        "sln_g": sgu_ln_g, "sln_b": sgu_ln_b,
        "sgu_w": jnp.transpose(sgu_w, (0, 2, 1, 3)).reshape(depth, SGU_LEN, SGU_GROUPS * SGU_LEN),
        "sgu_bias": jnp.repeat(jnp.transpose(sgu_b, (0, 2, 1)), BRANCH_W // SGU_GROUPS, axis=2),
        "ln_g": ln2_g, "ln_b": ln2_b,
    }


def kernel(x, ffn1_w1, ffn1_w2, ln1_g, ln1_b, w_in, conv_w, pool_w, pool_scale, ret_gn_g, ret_gn_b,
           sgu_ln_g, sgu_ln_b, sgu_w, sgu_b, w_branch, w_gate, b_gate, w_out, ln2_g, ln2_b,
           ffn2_w1, ffn2_w2, ln3_g, ln3_b):
    bsz, seq, d = x.shape
    depth = ffn1_w1.shape[0]
    mp = _prep_mixer_params(conv_w, pool_w, pool_scale, ret_gn_g, ret_gn_b, sgu_ln_g, sgu_ln_b,
                            sgu_w, sgu_b, b_gate, ln2_g, ln2_b)
    consts = _rope_tables(seq) + (_pool_inverse_counts(),) + _retention_constants()
    mixer_stacks = (w_in, w_gate, w_branch.reshape(depth, N_BRANCH * BRANCH_W, d), w_out)
    ffn1_stacks = (ffn1_w1, ffn1_w2)
    ffn2_stacks = (ffn2_w1, ffn2_w2)
    ffn_scales = (FFN_W1_SCALE, FFN_W2_SCALE)
    f1 = (_prescale(ffn1_w1[0], FFN_W1_SCALE).astype(BF16),
          _prescale(ffn1_w2[0], FFN_W2_SCALE).astype(BF16))
    x2 = x.reshape(bsz * seq, d)
    for l in range(depth):
        x2, mw = _ffn_layer(x2, f1[0], f1[1], ln1_g, ln1_b, l, mixer_stacks, MIXER_SCALES, l)
        x3, f2 = _mixer_layer(x2.reshape(bsz, seq, d), mw, mp, consts, l,
                              ffn2_stacks, ffn_scales, l)
        last = l + 1 == depth
        x2, f1 = _ffn_layer(x3.reshape(bsz * seq, d), f2[0], f2[1], ln3_g, ln3_b, l,
                            () if last else ffn1_stacks, () if last else ffn_scales, l + 1)
    return x2.reshape(bsz, seq, d)
```

```python
import functools
import math

import jax
import jax.numpy as jnp
import numpy as np
from jax import lax
from jax.experimental import pallas as pl
from jax.experimental.pallas import tpu as pltpu

F32 = jnp.float32
BF16 = jnp.bfloat16

D_MODEL = 1024
DEPTH = 4
CHUNK = 64
BRANCH_W = 256
N_BRANCH = 4
POOL_WINDOWS = (2, 4, 8, 16)
POOL_GC = BRANCH_W // len(POOL_WINDOWS)
RET_HEADS = 4
RET_DK = BRANCH_W // RET_HEADS
ROPE_BASE = 10000.0
SGU_LEN = 128
SGU_GROUPS = 4
D_FF = 2816
ALPHA = (2.0 * DEPTH) ** 0.25
LN_EPS = 1e-5
GN_EPS = 1e-5
IN_COLS = 10 * BRANCH_W

COL_BG, COL_CG, COL_XIN, COL_POOL = 0, 256, 512, 768
COL_Q, COL_K, COL_V, COL_G = 1024, 1280, 1536, 1792
COL_SU, COL_SV = 2048, 2304

FFN_SUBS = (512, 512)
FFN_TM = sum(FFN_SUBS)
MIX_SUBS = (512,)
MIX_TS = sum(MIX_SUBS)
RET_CC = 128
CONV_HIST = 8
ROPE_LANES = 128
POOL_HIST = 16
MERGE_COLS = 512
FFN_F_CHUNKS = ((0, 1536), (1536, 2816))
CAST_BLOCKS = 16
VMEM_LIMIT = 56 * 1024 * 1024


def _dot(a, b):
    return jnp.dot(a, b, preferred_element_type=F32)


def _layer_norm_rows(y, g, b, eps):
    mu = jnp.mean(y, axis=-1, keepdims=True)
    d = y - mu
    var = jnp.mean(d * d, axis=-1, keepdims=True)
    return d * lax.rsqrt(var + eps) * g + b


def _cast_io(stack, layer, steps, step_of):
    _, rows, cols = stack.shape
    blocks = math.gcd(steps, CAST_BLOCKS)
    slab = rows // blocks
    assert slab * blocks == rows and slab % 16 == 0
    per = steps // blocks
    in_spec = pl.BlockSpec((None, slab, cols), lambda *g: (layer, step_of(*g) // per, 0))
    out_spec = pl.BlockSpec((slab, cols), lambda *g: (step_of(*g) // per, 0))
    return in_spec, out_spec, jax.ShapeDtypeStruct((rows, cols), BF16)


def _prescale(w, scale):
    if isinstance(scale, tuple):
        split, left, right = scale
        col = lax.broadcasted_iota(jnp.int32, (1, w.shape[-1]), 1)
        return w * jnp.where(col < split, left, right).astype(w.dtype)
    return w if scale == 1.0 else w * scale


def _convert_slabs(src_refs, dst_refs, scales):
    for s_ref, d_ref, scale in zip(src_refs, dst_refs, scales, strict=True):
        d_ref[...] = _prescale(s_ref[...], scale).astype(BF16)


FFN_W1_SCALE = (D_FF, 0.5, 1.0)
FFN_W2_SCALE = 0.5
MIXER_SCALES = (1.0, 0.5, 0.5, 1.0)


def _ffn_kernel(cast_scales, layer, x_ref, w1_ref, w2_ref, g_ref, b_ref, *refs):
    n_cast = len(cast_scales)
    cast_src, o_ref, cast_dst = refs[:n_cast], refs[n_cast], refs[n_cast + 1:]
    _convert_slabs(cast_src, cast_dst, cast_scales)
    ln_g = g_ref[layer:layer + 1, :]
    ln_b = b_ref[layer:layer + 1, :]
    lo = 0
    for rows_n in FFN_SUBS:
        rows = slice(lo, lo + rows_n)
        lo += rows_n
        x = x_ref[rows, :]
        xb = x.astype(BF16)
        ffn = None
        for c_lo, c_hi in FFN_F_CHUNKS:
            half_gate = _dot(xb, w1_ref[:, c_lo:c_hi])
            up = _dot(xb, w1_ref[:, D_FF + c_lo:D_FF + c_hi])
            act = (half_gate * (1.0 + jnp.tanh(half_gate)) * up).astype(BF16)
            part = _dot(act, w2_ref[c_lo:c_hi, :])
            ffn = part if ffn is None else ffn + part
        y = ALPHA * x + ffn
        o_ref[rows, :] = _layer_norm_rows(y, ln_g, ln_b, LN_EPS)


def _ffn_layer(x2d, w1, w2, g_all, b_all, layer, cast_stacks, cast_scales, cast_layer):
    m = x2d.shape[0]
    steps = m // FFN_TM
    whole = lambda i: (0, 0)
    single = pl.Buffered(1)
    casts = [_cast_io(s, cast_layer, steps, lambda i: i) for s in cast_stacks]
    outs = pl.pallas_call(
        functools.partial(_ffn_kernel, tuple(cast_scales), layer),
        grid=(steps,),
        in_specs=[
            pl.BlockSpec((FFN_TM, D_MODEL), lambda i: (i, 0)),
            pl.BlockSpec((D_MODEL, 2 * D_FF), whole, pipeline_mode=single),
            pl.BlockSpec((D_FF, D_MODEL), whole, pipeline_mode=single),
            pl.BlockSpec(g_all.shape, whole, pipeline_mode=single),
            pl.BlockSpec(b_all.shape, whole, pipeline_mode=single),
        ] + [c[0] for c in casts],
        out_specs=[pl.BlockSpec((FFN_TM, D_MODEL), lambda i: (i, 0))] + [c[1] for c in casts],
        out_shape=[jax.ShapeDtypeStruct((m, D_MODEL), F32)] + [c[2] for c in casts],
        compiler_params=pltpu.CompilerParams(
            dimension_semantics=("arbitrary",), vmem_limit_bytes=VMEM_LIMIT),
        name="ffn_ln",
    )(x2d, w1, w2, g_all, b_all, *cast_stacks)
    return outs[0], outs[1:]


def _lane_group_mask(shape, group):
    lane = lax.broadcasted_iota(jnp.int32, shape, 1)
    return (lane // RET_DK) == group


def _stack_masked(x):
    return jnp.concatenate(
        [jnp.where(_lane_group_mask(x.shape, g), x, 0.0) for g in range(RET_HEADS)], axis=0)


def _head_columns(x_t):
    rows, cc = x_t.shape
    blocks = []
    for g in range(RET_HEADS):
        lo, hi = g * RET_DK, (g + 1) * RET_DK
        parts = [jnp.zeros((lo, cc), x_t.dtype)] if lo else []
        parts.append(x_t[lo:hi, :])
        if hi < rows:
            parts.append(jnp.zeros((rows - hi, cc), x_t.dtype))
        blocks.append(jnp.concatenate(parts, axis=0))
    return jnp.concatenate(blocks, axis=1)


def _group_mean(x, avg_ref):
    hi = x.astype(BF16)
    lo = (x - hi.astype(F32)).astype(BF16)
    return _dot(jnp.concatenate([hi, lo], axis=1), avg_ref[...])


def _add_gated_branch(acc_ref, rows, y, hb, n, w_gate_ref, half_b_gate, w_branch_ref):
    yb = y.astype(BF16)
    for lo in range(0, D_MODEL, MERGE_COLS):
        cols = slice(lo, lo + MERGE_COLS)
        gcols = slice(n * D_MODEL + lo, n * D_MODEL + lo + MERGE_COLS)
        gate2 = 1.0 + jnp.tanh(_dot(hb, w_gate_ref[:, gcols]) + half_b_gate[:, gcols])
        term = gate2 * _dot(yb, w_branch_ref[n * BRANCH_W:(n + 1) * BRANCH_W, cols])
        if n == 0:
            acc_ref[rows, cols] = term
        else:
            acc_ref[rows, cols] += term


def _mixer_kernel(cast_scales, layer,
                  h_ref, w_in_ref, w_gate_ref, b_gate_ref, w_branch_ref, w_out_ref,
                  conv_w_ref, pool_bd_ref, pool_scale_ref, gn_g_ref, gn_b_ref,
                  sln_g_ref, sln_b_ref, sgu_w_ref, sgu_bias_ref,
                  cos_ref, sin_ref, icnt_ref, dcat_ref, xi_ref, zeta_ref, cd_ref, bm_ref, avg_ref,
                  ln_g_ref, ln_b_ref, *refs):
    n_cast = len(cast_scales)
    cast_src, o_ref, cast_dst = refs[:n_cast], refs[n_cast], refs[n_cast + 1:2 * n_cast + 1]
    (z_ref, cext_ref, pext_ref, q_ref, k_ref, ret_ref, state_ref,
     merged_ref) = refs[2 * n_cast + 1:]
    ts = h_ref.shape[0]
    s = pl.program_id(1)

    _convert_slabs(cast_src, cast_dst, cast_scales)

    @pl.when(s == 0)
    def _():
        cext_ref[0:CONV_HIST, :] = jnp.zeros((CONV_HIST, BRANCH_W), F32)
        pext_ref[0:POOL_HIST, :] = jnp.zeros((POOL_HIST, BRANCH_W), F32)
        state_ref[...] = jnp.zeros_like(state_ref)

    wi = lax.broadcasted_iota(jnp.int32, (SGU_LEN, SGU_GROUPS * SGU_LEN), 0)
    wj = lax.broadcasted_iota(jnp.int32, (SGU_LEN, SGU_GROUPS * SGU_LEN), 1) % SGU_LEN
    w_s = jnp.where((wj // CHUNK) <= (wi // CHUNK), sgu_w_ref[...], 0.0).astype(BF16)
    cw = conv_w_ref[...]
    cd = cd_ref[...]
    bm = bm_ref[...]
    state = state_ref[...]
    layer_row = lambda ref: ref[layer:layer + 1, :]
    b_gate, pool_scale = 0.5 * layer_row(b_gate_ref), layer_row(pool_scale_ref)
    gn_g, gn_b = layer_row(gn_g_ref), layer_row(gn_b_ref)
    sln_g, sln_b = layer_row(sln_g_ref), layer_row(sln_b_ref)
    ln_g, ln_b = layer_row(ln_g_ref), layer_row(ln_b_ref)

    lo = 0
    for j, rn in enumerate(MIX_SUBS):
        rows = slice(lo, lo + rn)
        wr = slice((j % 2) * rn, (j % 2 + 1) * rn)
        h = h_ref[rows, :]
        hb = h.astype(BF16)
        z_ref[wr, :] = _dot(hb, w_in_ref[...])

        c = z_ref[wr, COL_CG:COL_CG + BRANCH_W] * z_ref[wr, COL_XIN:COL_XIN + BRANCH_W]
        at = CONV_HIST + lo
        cext_ref[at:at + rn, :] = c
        conv = (cw[0:1, :] * cext_ref[at - 2:at - 2 + rn, :]
                + cw[1:2, :] * cext_ref[at - 1:at - 1 + rn, :]
                + cw[2:3, :] * c)
        y_a = z_ref[wr, COL_BG:COL_BG + BRANCH_W] * conv
        _add_gated_branch(merged_ref, wr, y_a, hb, 0, w_gate_ref, b_gate, w_branch_ref)

        zp = z_ref[wr, COL_POOL:COL_POOL + BRANCH_W]
        pext_ref[POOL_HIST + lo:POOL_HIST + lo + rn, :] = zp
        half = BRANCH_W // 2
        low_group = lax.broadcasted_iota(jnp.int32, (rn, half), 1) < POOL_GC
        ext = pext_ref[lo:lo + POOL_HIST + rn, 0:half]
        s2 = ext + pltpu.roll(ext, 1, 0)
        s4 = s2 + pltpu.roll(s2, 2, 0)
        wsum_lo = jnp.where(low_group, s2[POOL_HIST:], s4[POOL_HIST:])
        ext = pext_ref[lo:lo + POOL_HIST + rn, half:BRANCH_W]
        s2 = ext + pltpu.roll(ext, 1, 0)
        s4 = s2 + pltpu.roll(s2, 2, 0)
        s8 = s4 + pltpu.roll(s4, 4, 0)
        s16 = s8 + pltpu.roll(s8, 8, 0)
        wsum_hi = jnp.where(low_group, s8[POOL_HIST:], s16[POOL_HIST:])
        wsum = jnp.concatenate([wsum_lo, wsum_hi], axis=1)
        lane = lax.broadcasted_iota(jnp.int32, (rn, BRANCH_W), 1)
        mixed = wsum * icnt_ref[POOL_HIST:POOL_HIST + 1, :] - zp
        if j == 0:
            inv_cnt = jnp.where(s == 0, icnt_ref[0:POOL_HIST, :], icnt_ref[POOL_HIST:, :])
            mixed = jnp.concatenate(
                [wsum[0:POOL_HIST, :] * inv_cnt - zp[0:POOL_HIST, :], mixed[POOL_HIST:, :]], axis=0)
        y_p = _dot(mixed.astype(BF16), pool_bd_ref[...]) * pool_scale
        _add_gated_branch(merged_ref, wr, y_p, hb, 1, w_gate_ref, b_gate, w_branch_ref)

        cos = jnp.concatenate([cos_ref[rows, :]] * (BRANCH_W // ROPE_LANES), axis=1)
        sin = jnp.concatenate([sin_ref[rows, :]] * (BRANCH_W // ROPE_LANES), axis=1)
        first_half = (lane % RET_DK) < (RET_DK // 2)

        def rope(x):
            swapped = jnp.where(first_half, pltpu.roll(x, BRANCH_W - RET_DK // 2, 1),
                                pltpu.roll(x, RET_DK // 2, 1))
            return x * cos + swapped * sin

        q_ref[wr, :] = rope(z_ref[wr, COL_Q:COL_Q + BRANCH_W])
        k_ref[wr, :] = rope(z_ref[wr, COL_K:COL_K + BRANCH_W]) * (RET_DK ** -0.5)

        for n in range(rn // RET_CC):
            rr = slice(wr.start + n * RET_CC, wr.start + (n + 1) * RET_CC)
            qc = q_ref[rr, :]
            kc = k_ref[rr, :]
            vc = z_ref[rr, COL_V:COL_V + BRANCH_W]
            k_t = jnp.transpose(kc)
            v_stack = _stack_masked(vc).astype(BF16)
            scores = _dot(qc.astype(BF16), _head_columns(k_t).astype(BF16))
            probs = (scores * dcat_ref[...]).astype(BF16)
            inner = _dot(probs, v_stack)
            cross = _dot((qc * xi_ref[...]).astype(BF16), state.astype(BF16))
            ret_ref[rr, :] = inner + cross
            kv = _dot((k_t * zeta_ref[...]).astype(BF16), vc.astype(BF16))
            state = state * cd + kv * bm

        o = ret_ref[wr, :]
        mu = _group_mean(o, avg_ref)
        d = o - mu
        var = _group_mean(d * d, avg_ref)
        o_n = d * lax.rsqrt(var + GN_EPS) * gn_g + gn_b
        gate_r = z_ref[wr, COL_G:COL_G + BRANCH_W]
        y_r = gate_r * jax.nn.sigmoid(gate_r) * o_n
        _add_gated_branch(merged_ref, wr, y_r, hb, 2, w_gate_ref, b_gate, w_branch_ref)

        u = jax.nn.gelu(z_ref[wr, COL_SU:COL_SU + BRANCH_W])
        v = jax.nn.gelu(z_ref[wr, COL_SV:COL_SV + BRANCH_W])
        v = _layer_norm_rows(v, sln_g, sln_b, LN_EPS)
        pieces = []
        for n in range(rn // SGU_LEN):
            v_stack = _stack_masked(v[n * SGU_LEN:(n + 1) * SGU_LEN, :]).astype(BF16)
            pieces.append(_dot(w_s, v_stack) + sgu_bias_ref[...])
        y_s = u * jnp.concatenate(pieces, axis=0)
        _add_gated_branch(merged_ref, wr, y_s, hb, 3, w_gate_ref, b_gate, w_branch_ref)
        merged = merged_ref[wr, :]

        out = _dot(merged.astype(BF16), w_out_ref[...])
        o_ref[rows, :] = _layer_norm_rows(ALPHA * h + out, ln_g, ln_b, LN_EPS)
        lo += rn

    cext_ref[0:CONV_HIST, :] = cext_ref[ts:ts + CONV_HIST, :]
    pext_ref[0:POOL_HIST, :] = pext_ref[ts:ts + POOL_HIST, :]
    state_ref[...] = state


def _retention_constants():
    cc = RET_CC
    heads = np.arange(RET_HEADS, dtype=np.float64)
    log_gamma = np.log1p(-(2.0 ** (-5.0 - heads)))
    idx = np.arange(cc, dtype=np.float64)
    diff = idx[:, None] - idx[None, :]
    decay = np.where(diff >= 0,
                     np.exp(log_gamma[:, None, None] * np.maximum(diff, 0.0)), 0.0)
    dcat = np.transpose(decay, (1, 0, 2)).reshape(cc, RET_HEADS * cc)
    zeta = np.exp(log_gamma[:, None] * (cc - 1 - idx)[None, :])
    xi = np.exp(log_gamma[:, None] * (idx + 1.0)[None, :])
    chunk_decay = np.exp(log_gamma * cc)
    zeta_l = np.repeat(zeta.T, RET_DK, axis=1)
    xi_l = np.repeat(xi.T, RET_DK, axis=1)
    head_of = np.arange(BRANCH_W) // RET_DK
    same = head_of[:, None] == head_of[None, :]
    bm = same.astype(np.float64)
    cd = np.where(same, chunk_decay[head_of][:, None], 0.0)
    f32 = lambda a: jnp.asarray(a, F32)
    return (f32(dcat), f32(xi_l), f32(zeta_l.T), f32(cd), f32(bm),
            jnp.asarray(np.concatenate([bm, bm], axis=0) / RET_DK, BF16))


def _rope_tables(seq):
    half = RET_DK // 2
    inv = ROPE_BASE ** (-jnp.arange(half, dtype=F32) / half)
    ang = jnp.arange(seq).astype(F32)[:, None] * inv[None, :]
    cos = jnp.cos(ang)
    sin = jnp.sin(ang)
    reps = ROPE_LANES // RET_DK
    cos_l = jnp.tile(jnp.concatenate([cos, cos], axis=1), (1, reps))
    sin_l = jnp.tile(jnp.concatenate([-sin, sin], axis=1), (1, reps))
    return cos_l, sin_l


def _pool_inverse_counts():
    assert POOL_HIST >= max(POOL_WINDOWS)
    win = np.repeat(np.asarray(POOL_WINDOWS, np.float64), POOL_GC)
    head = np.minimum(np.arange(POOL_HIST)[:, None] + 1.0, win[None, :])
    steady = np.broadcast_to(win[None, :], (POOL_HIST, BRANCH_W))
    return jnp.asarray(1.0 / np.concatenate([head, steady], axis=0), F32)


def _mixer_layer(x3d, wts, p, consts, layer, cast_stacks, cast_scales, cast_layer):
    bsz, seq, _ = x3d.shape
    ts = MIX_TS
    nseq = seq // ts
    single = pl.Buffered(1)

    def lspec(shape):
        nd = len(shape)
        return pl.BlockSpec((None,) + shape, lambda b, s: (layer,) + (0,) * nd,
                            pipeline_mode=single)

    def cspec(shape):
        nd = len(shape)
        return pl.BlockSpec(shape, lambda b, s: (0,) * nd, pipeline_mode=single)

    casts = [_cast_io(st, cast_layer, bsz * nseq, lambda b, s: b * nseq + s) for st in cast_stacks]
    tile = pl.BlockSpec((None, ts, D_MODEL), lambda b, s: (b, s, 0))
    table = pl.BlockSpec((ts, ROPE_LANES), lambda b, s: (s, 0))
    cos_l, sin_l, icnt, dcat, xi_l, zeta_l, cd, bm, avg = consts
    w_in, w_gate, w_branch, w_out = wts
    in_specs = [
        tile,
        cspec(w_in.shape), cspec(w_gate.shape), cspec(p["b_gate"].shape),
        cspec(w_branch.shape), cspec(w_out.shape),
        lspec((3, BRANCH_W)), lspec((BRANCH_W, BRANCH_W)), cspec(p["pool_scale"].shape),
        cspec(p["gn_g"].shape), cspec(p["gn_b"].shape),
        cspec(p["sln_g"].shape), cspec(p["sln_b"].shape),
        lspec((SGU_LEN, SGU_GROUPS * SGU_LEN)), lspec((SGU_LEN, BRANCH_W)),
        table, table, cspec(icnt.shape),
        cspec(dcat.shape), cspec(xi_l.shape), cspec(zeta_l.shape),
        cspec(cd.shape), cspec(bm.shape), cspec(avg.shape),
        cspec(p["ln_g"].shape), cspec(p["ln_b"].shape),
    ] + [c[0] for c in casts]
    assert len(set(MIX_SUBS)) == 1, "alternating work scratch needs equal sub-tiles"
    work = min(len(MIX_SUBS), 2) * MIX_SUBS[0]
    scratch = [
        pltpu.VMEM((work, IN_COLS), F32),
        pltpu.VMEM((CONV_HIST + ts, BRANCH_W), F32),
        pltpu.VMEM((POOL_HIST + ts, BRANCH_W), F32),
        pltpu.VMEM((work, BRANCH_W), F32),
        pltpu.VMEM((work, BRANCH_W), F32),
        pltpu.VMEM((work, BRANCH_W), F32),
        pltpu.VMEM((BRANCH_W, BRANCH_W), F32),
        pltpu.VMEM((work, D_MODEL), F32),
    ]
    outs = pl.pallas_call(
        functools.partial(_mixer_kernel, tuple(cast_scales), layer),
        grid=(bsz, nseq),
        in_specs=in_specs,
        out_specs=[tile] + [c[1] for c in casts],
        out_shape=[jax.ShapeDtypeStruct((bsz, seq, D_MODEL), F32)] + [c[2] for c in casts],
        scratch_shapes=scratch,
        compiler_params=pltpu.CompilerParams(
            dimension_semantics=("arbitrary", "arbitrary"), vmem_limit_bytes=VMEM_LIMIT),
        name="mixer_ln",
    )(x3d, w_in, w_gate, p["b_gate"], w_branch, w_out,
      p["conv_w"], p["pool_bd"], p["pool_scale"], p["gn_g"], p["gn_b"],
      p["sln_g"], p["sln_b"], p["sgu_w"], p["sgu_bias"],
      cos_l, sin_l, icnt, dcat, xi_l, zeta_l, cd, bm, avg, p["ln_g"], p["ln_b"], *cast_stacks)
    return outs[0], outs[1:]


def _prep_mixer_params(conv_w, pool_w, pool_scale, ret_gn_g, ret_gn_b, sgu_ln_g, sgu_ln_b,
                       sgu_w, sgu_b, b_gate, ln2_g, ln2_b):
    depth = conv_w.shape[0]
    groups = pool_w.shape[1]
    eye = jnp.eye(groups, dtype=pool_w.dtype)
    pool_bd = jnp.einsum("lgcd,gh->lgchd", pool_w, eye).reshape(depth, BRANCH_W, BRANCH_W)
    return {
        "b_gate": b_gate,
        "conv_w": conv_w, "pool_bd": pool_bd.astype(BF16), "pool_scale": pool_scale,
        "gn_g": ret_gn_g, "gn_b": ret_gn_b,
        "sln_g": sgu_ln_g, "sln_b": sgu_ln_b,
        "sgu_w": jnp.transpose(sgu_w, (0, 2, 1, 3)).reshape(depth, SGU_LEN, SGU_GROUPS * SGU_LEN),
        "sgu_bias": jnp.repeat(jnp.transpose(sgu_b, (0, 2, 1)), BRANCH_W // SGU_GROUPS, axis=2),
        "ln_g": ln2_g, "ln_b": ln2_b,
    }


def kernel(x, ffn1_w1, ffn1_w2, ln1_g, ln1_b, w_in, conv_w, pool_w, pool_scale, ret_gn_g, ret_gn_b,
           sgu_ln_g, sgu_ln_b, sgu_w, sgu_b, w_branch, w_gate, b_gate, w_out, ln2_g, ln2_b,
           ffn2_w1, ffn2_w2, ln3_g, ln3_b):
    bsz, seq, d = x.shape
    depth = ffn1_w1.shape[0]
    mp = _prep_mixer_params(conv_w, pool_w, pool_scale, ret_gn_g, ret_gn_b, sgu_ln_g, sgu_ln_b,
                            sgu_w, sgu_b, b_gate, ln2_g, ln2_b)
    consts = _rope_tables(seq) + (_pool_inverse_counts(),) + _retention_constants()
    mixer_stacks = (w_in, w_gate, w_branch.reshape(depth, N_BRANCH * BRANCH_W, d), w_out)
    ffn1_stacks = (ffn1_w1, ffn1_w2)
    ffn2_stacks = (ffn2_w1, ffn2_w2)
    ffn_scales = (FFN_W1_SCALE, FFN_W2_SCALE)
    f1 = (_prescale(ffn1_w1[0], FFN_W1_SCALE).astype(BF16),
          _prescale(ffn1_w2[0], FFN_W2_SCALE).astype(BF16))
    x2 = x.reshape(bsz * seq, d)
    for l in range(depth):
        x2, mw = _ffn_layer(x2, f1[0], f1[1], ln1_g, ln1_b, l, mixer_stacks, MIXER_SCALES, l)
        x3, f2 = _mixer_layer(x2.reshape(bsz, seq, d), mw, mp, consts, l,
                              ffn2_stacks, ffn_scales, l)
        last = l + 1 == depth
        x2, f1 = _ffn_layer(x3.reshape(bsz * seq, d), f2[0], f2[1], ln3_g, ln3_b, l,
                            () if last else ffn1_stacks, () if last else ffn_scales, l + 1)
    return x2.reshape(bsz, seq, d)
```
